```python
import math
import jax
import jax.numpy as jnp
from jax import lax
import numpy as np

D_MODEL = 1024
BATCH = 4
SEQ = 8192
DEPTH = 1
DEC_BATCH = 128
DEC_SEQ = 1
PAST_LEN = 8192
PAGE_SIZE = 128

H_A = 4
DK_A = D_MODEL // 8
DV_A = D_MODEL // 8
W_A = H_A * DV_A
QKV_A = 2 * H_A * DK_A + H_A * DV_A
CONV_W = 4
CHUNK = 64
H_B = 4
D_B = D_MODEL // 16
W_B = H_B * 2 * D_B
ROPE_DIM = D_B // 4
ROPE_THETA = 500000.0
Q_BLOCK = 128
MIX_WIDTH = W_A + W_B
IN_COLS = (QKV_A, W_A, H_A, H_A, W_B, W_B, W_B)
N_IN_COLS = QKV_A + W_A + 2 * H_A + 3 * W_B
N_GROUPS = 4
E_PER_GROUP = 8
TOP_K_FINE = 2
D_EXPERT = D_MODEL // 4
DN_ALPHA = (2.0 * DEPTH) ** 0.25
DN_BETA = (8.0 * DEPTH) ** -0.25
LN_EPS = 1e-5
RMS_EPS = 1e-6

kernel_name = 'hybrid_gdn_diffattn_hmoe_step'


def layer_norm(x, g, b):
    xf = x.astype(jnp.float32)
    mu = jnp.mean(xf, axis=-1, keepdims=True)
    var = jnp.mean(jnp.square(xf - mu), axis=-1, keepdims=True)
    return ((xf - mu) * lax.rsqrt(var + LN_EPS) * g + b).astype(x.dtype)


def rms_norm(x, w):
    xf = x.astype(jnp.float32)
    return (xf * lax.rsqrt(jnp.mean(jnp.square(xf), axis=-1, keepdims=True) + RMS_EPS) * w).astype(x.dtype)


def l2_normalize(x):
    xf = x.astype(jnp.float32)
    return xf * lax.rsqrt(jnp.sum(jnp.square(xf), axis=-1, keepdims=True) + RMS_EPS)


def partial_rope(x, pos):
    half = ROPE_DIM // 2
    inv = ROPE_THETA ** (-jnp.arange(0, ROPE_DIM, 2, dtype=jnp.float32) / ROPE_DIM)
    ang = pos.astype(jnp.float32)[:, None] * inv
    cos = jnp.cos(ang)[:, None, None, :]
    sin = jnp.sin(ang)[:, None, None, :]
    x1 = x[..., :half].astype(jnp.float32)
    x2 = x[..., half:ROPE_DIM].astype(jnp.float32)
    rot = jnp.concatenate([x1 * cos - x2 * sin, x2 * cos + x1 * sin], axis=-1)
    return jnp.concatenate([rot.astype(x.dtype), x[..., ROPE_DIM:]], axis=-1)


def causal_short_conv(x, buf, w):
    t = x.shape[1]
    xp = jnp.concatenate([buf.astype(x.dtype), x], axis=1)
    y = sum(xp[:, i:i + t] * w[i] for i in range(CONV_W))
    return jax.nn.silu(y), xp[:, t:]


def gated_delta_chunked(q, k, v, g, beta, s0):
    b, t, h, _ = q.shape
    dv = v.shape[-1]
    n_pad = (-t) % CHUNK
    nc = (t + n_pad) // CHUNK

    def blocks(a):
        a = jnp.pad(a.astype(jnp.float32), [(0, 0), (0, n_pad)] + [(0, 0)] * (a.ndim - 2))
        a = a.reshape((b, nc, CHUNK) + a.shape[2:])
        return jnp.moveaxis(a, 3, 1)

    q, k, v, g, beta = blocks(q), blocks(k), blocks(v), blocks(g), blocks(beta)
    gam = jnp.cumsum(g, axis=-1)
    incl = jnp.tril(jnp.ones((CHUNK, CHUNK), bool))
    strict = jnp.tril(jnp.ones((CHUNK, CHUNK), bool), -1)
    diff = gam[..., :, None] - gam[..., None, :]
    decay = jnp.where(incl, jnp.exp(jnp.where(incl, diff, 0.0)), 0.0)
    kk = jnp.einsum('bhnid,bhnjd->bhnij', k, k)
    a_mat = jnp.eye(CHUNK, dtype=jnp.float32) + jnp.where(strict, beta[..., :, None] * kk * decay, 0.0)
    rhs = jnp.concatenate([beta[..., None] * v, (beta * jnp.exp(gam))[..., None] * k], axis=-1)
    sol = lax.linalg.triangular_solve(a_mat, rhs, left_side=True, lower=True, unit_diagonal=True)
    w_v, w_k = sol[..., :dv], sol[..., dv:]
    attn = jnp.einsum('bhnid,bhnjd->bhnij', q, k) * decay
    q_dec = jnp.exp(gam)[..., None] * q
    k_end = jnp.exp(gam[..., -1:] - gam)[..., None] * k
    g_end = jnp.exp(gam[..., -1])

    def step(s, xs):
        w_v_c, w_k_c, q_c, a_c, k_c, ge_c = xs
        u = w_v_c - jnp.einsum('bhik,bhkv->bhiv', w_k_c, s)
        o = jnp.einsum('bhik,bhkv->bhiv', q_c, s) + jnp.einsum('bhij,bhjv->bhiv', a_c, u)
        s = ge_c[..., None, None] * s + jnp.einsum('bhik,bhiv->bhkv', k_c, u)
        return s, o

    xs = tuple(jnp.moveaxis(a, 2, 0) for a in (w_v, w_k, q_dec, attn, k_end, g_end))
    s_final, o = lax.scan(step, s0.astype(jnp.float32), xs)
    o = jnp.moveaxis(o, 0, 2).reshape(b, h, nc * CHUNK, dv)[:, :, :t]
    return jnp.swapaxes(o, 1, 2), s_final.astype(s0.dtype)


def diff_attend(q, k, v, q_pos, k_pos, lam):
    s = jnp.einsum('bqhmd,bkhmd->bhmqk', q, k).astype(jnp.float32) * (D_B ** -0.5)
    s = jnp.where(k_pos[None, :] <= q_pos[:, None], s, -jnp.inf)
    p = jax.nn.softmax(s, axis=-1)
    w = p[:, :, 0] - lam * p[:, :, 1]
    return jnp.einsum('bhqk,bkhe->bqhe', w.astype(v.dtype), v)


def token_mixer(x, pos, conv_buf, s0, k_past, v_past, lam, lam_init,
                w_in, conv_w, a_log, dt_bias, gdn_norm_w, diff_norm_w, w_out):
    b, t, _ = x.shape
    proj = jnp.einsum('bsd,dc->bsc', x, w_in)
    split_at = np.cumsum(IN_COLS)[:-1].tolist()
    qkv_a, z_a, beta_a, a_a, q_b, k_b, v_b = jnp.split(proj, split_at, axis=-1)
    qkv_a, new_conv = causal_short_conv(qkv_a, conv_buf, conv_w)
    q_a, k_a, v_a = jnp.split(qkv_a, [H_A * DK_A, 2 * H_A * DK_A], axis=-1)
    q_a = l2_normalize(q_a.reshape(b, t, H_A, DK_A)) * (DK_A ** -0.5)
    k_a = l2_normalize(k_a.reshape(b, t, H_A, DK_A))
    v_a = v_a.reshape(b, t, H_A, DV_A)
    beta = jax.nn.sigmoid(beta_a.astype(jnp.float32))
    g = -jnp.exp(a_log.astype(jnp.float32)) * jax.nn.softplus(a_a.astype(jnp.float32) + dt_bias)
    o_a, s_new = gated_delta_chunked(q_a, k_a, v_a, g, beta, s0)
    o_a = rms_norm(o_a, gdn_norm_w) * jax.nn.silu(z_a.reshape(b, t, H_A, DV_A).astype(jnp.float32))
    q_b = partial_rope(q_b.reshape(b, t, H_B, 2, D_B), pos)
    k_b = partial_rope(k_b.reshape(b, t, H_B, 2, D_B), pos)
    v_b = v_b.reshape(b, t, H_B, 2 * D_B)
    if k_past is None:
        nqb = t // Q_BLOCK
        q_blocks = jnp.moveaxis(q_b.reshape(b, nqb, Q_BLOCK, H_B, 2, D_B), 1, 0)
        starts = jnp.arange(nqb) * Q_BLOCK

        def one_block(args):
            qi, st = args
            return diff_attend(qi, k_b, v_b, st + jnp.arange(Q_BLOCK), pos, lam)

        o_b = jnp.moveaxis(lax.map(one_block, (q_blocks, starts)), 0, 1).reshape(b, t, H_B, 2 * D_B)
    else:
        past = k_past.shape[1]
        k_all = jnp.concatenate([k_past.reshape(b, past, H_B, 2, D_B).astype(k_b.dtype), k_b], axis=1)
        v_all = jnp.concatenate([v_past.astype(v_b.dtype), v_b], axis=1)
        o_b = diff_attend(q_b, k_all, v_all, pos, jnp.arange(past + t), lam)
    o_b = rms_norm(o_b, diff_norm_w) * (1.0 - lam_init)
    mixed = jnp.concatenate([o_a.reshape(b, t, W_A).astype(x.dtype), o_b.reshape(b, t, W_B).astype(x.dtype)], axis=-1)
    y = jnp.einsum('bsc,cd->bsd', mixed, w_out)
    return y, k_b.reshape(b, t, H_B, 2 * D_B), v_b, s_new, new_conv


def hierarchical_moe(x, w_rg, b_rg, w_re, b_re, w_gate, w_up, w_down):
    shp = x.shape
    xt = x.reshape(-1, D_MODEL)
    g_logits = (xt @ w_rg + b_rg).astype(jnp.float32)
    g_prob = jax.nn.softmax(g_logits, axis=-1)
    g_idx = jnp.argmax(g_logits, axis=-1)
    g_w = jnp.take_along_axis(g_prob, g_idx[:, None], axis=-1)
    e_logits = (xt @ w_re + b_re).astype(jnp.float32).reshape(-1, N_GROUPS, E_PER_GROUP)
    e_logits = jnp.take_along_axis(e_logits, g_idx[:, None, None], axis=1)[:, 0]
    top_v, top_i = lax.top_k(e_logits, TOP_K_FINE)
    top_w = jax.nn.softmax(top_v, axis=-1) * g_w
    e_gate = jnp.sum(jax.nn.one_hot(top_i, E_PER_GROUP, dtype=jnp.float32) * top_w[..., None], axis=1)
    gate = jax.nn.one_hot(g_idx, N_GROUPS, dtype=jnp.float32)[:, :, None] * e_gate[:, None, :]
    y = jnp.zeros_like(xt)
    for gi in range(N_GROUPS):
        h = jax.nn.silu(jnp.einsum('td,edf->tef', xt, w_gate[gi])) * jnp.einsum('td,edf->tef', xt, w_up[gi])
        y = y + jnp.einsum('tef,efd->td', h * gate[:, gi, :, None].astype(h.dtype), w_down[gi])
    return y.reshape(shp)


def setup_inputs(seed: int = 0) -> dict:
    key = jax.random.key(seed)
    ks = jax.random.split(key, 32)
    f32 = jnp.float32
    n_pages = PAST_LEN // PAGE_SIZE
    n_used = DEC_BATCH * n_pages
    n_pool = n_used + n_used // 4 + 1

    def nrm(k, shape, scale):
        return scale * jax.random.normal(k, shape, f32)

    def gain(k, shape):
        return 1.0 + nrm(k, shape, 0.02)

    page_table = jax.random.permutation(ks[6], n_pool)[:n_used].reshape(DEC_BATCH, n_pages).astype(jnp.int32)
    dt = jnp.exp(jax.random.uniform(ks[12], (DEPTH, H_A), f32, math.log(1e-3), math.log(1e-1)))
    return {
        'x_prompt': nrm(ks[0], (BATCH, SEQ, D_MODEL), 1.0),
        'x_sample': nrm(ks[1], (DEC_BATCH, DEC_SEQ, D_MODEL), 1.0),
        'cache_k': nrm(ks[2], (DEPTH, n_pool, PAGE_SIZE, H_B, 2 * D_B), 1.0),
        'cache_v': nrm(ks[3], (DEPTH, n_pool, PAGE_SIZE, H_B, 2 * D_B), 1.0),
        'state_delta': nrm(ks[4], (DEPTH, DEC_BATCH, H_A, DK_A, DV_A), 0.1),
        'state_conv': nrm(ks[5], (DEPTH, DEC_BATCH, CONV_W - 1, QKV_A), 1.0),
        'page_table': page_table,
        'ln_in_g': gain(ks[7], (D_MODEL,)),
        'ln_in_b': nrm(ks[8], (D_MODEL,), 0.02),
        'w_in': nrm(ks[9], (DEPTH, D_MODEL, N_IN_COLS), D_MODEL ** -0.5),
        'conv_w': nrm(ks[10], (DEPTH, CONV_W, QKV_A), CONV_W ** -0.5),
        'a_log': jnp.log(jax.random.uniform(ks[11], (DEPTH, H_A), f32, 1.0, 16.0)),
        'dt_bias': dt + jnp.log(-jnp.expm1(-dt)),
        'gdn_norm_w': gain(ks[13], (DEPTH, DV_A)),
        'lambda_q1': nrm(ks[14], (DEPTH, D_B), 0.1),
        'lambda_k1': nrm(ks[15], (DEPTH, D_B), 0.1),
        'lambda_q2': nrm(ks[16], (DEPTH, D_B), 0.1),
        'lambda_k2': nrm(ks[17], (DEPTH, D_B), 0.1),
        'diff_norm_w': gain(ks[18], (DEPTH, 2 * D_B)),
        'w_out': nrm(ks[19], (DEPTH, MIX_WIDTH, D_MODEL), DN_BETA * MIX_WIDTH ** -0.5),
        'ln1_g': gain(ks[20], (DEPTH, D_MODEL)),
        'ln1_b': nrm(ks[21], (DEPTH, D_MODEL), 0.02),
        'w_router_group': nrm(ks[22], (DEPTH, D_MODEL, N_GROUPS), D_MODEL ** -0.5),
        'b_router_group': nrm(ks[23], (DEPTH, N_GROUPS), 0.01),
        'w_router_expert': nrm(ks[24], (DEPTH, D_MODEL, N_GROUPS * E_PER_GROUP), D_MODEL ** -0.5),
        'b_router_expert': nrm(ks[25], (DEPTH, N_GROUPS * E_PER_GROUP), 0.01),
        'w_gate': nrm(ks[26], (DEPTH, N_GROUPS, E_PER_GROUP, D_MODEL, D_EXPERT), D_MODEL ** -0.5),
        'w_up': nrm(ks[27], (DEPTH, N_GROUPS, E_PER_GROUP, D_MODEL, D_EXPERT), D_MODEL ** -0.5),
        'w_down': nrm(ks[28], (DEPTH, N_GROUPS, E_PER_GROUP, D_EXPERT, D_MODEL), DN_BETA * D_EXPERT ** -0.5),
        'ln2_g': gain(ks[29], (DEPTH, D_MODEL)),
        'ln2_b': nrm(ks[30], (DEPTH, D_MODEL), 0.02),
    }


def reference(x_prompt, x_sample, cache_k, cache_v, state_delta, state_conv, page_table,
              ln_in_g, ln_in_b, w_in, conv_w, a_log, dt_bias, gdn_norm_w,
              lambda_q1, lambda_k1, lambda_q2, lambda_k2, diff_norm_w, w_out, ln1_g, ln1_b,
              w_router_group, b_router_group, w_router_expert, b_router_expert,
              w_gate, w_up, w_down, ln2_g, ln2_b):
    bp, tp, _ = x_prompt.shape
    bs, ts, _ = x_sample.shape
    n_pages = page_table.shape[1]
    past_len = n_pages * PAGE_SIZE
    pos_p = jnp.arange(tp)
    pos_s = past_len + jnp.arange(ts)
    xp = layer_norm(x_prompt, ln_in_g, ln_in_b)
    xs = layer_norm(x_sample, ln_in_g, ln_in_b)
    kp_l, vp_l, sp_l, cp_l = [], [], [], []
    ks_l, vs_l, ss_l, cs_l = [], [], [], []
    for l in range(DEPTH):
        lam_init = 0.8 - 0.6 * math.exp(-0.3 * l)
        lam = (jnp.exp(jnp.sum(lambda_q1[l].astype(jnp.float32) * lambda_k1[l].astype(jnp.float32)))
               - jnp.exp(jnp.sum(lambda_q2[l].astype(jnp.float32) * lambda_k2[l].astype(jnp.float32)))
               + lam_init)
        mix_w = (w_in[l], conv_w[l], a_log[l], dt_bias[l], gdn_norm_w[l], diff_norm_w[l], w_out[l])
        moe_w = (w_router_group[l], b_router_group[l], w_router_expert[l], b_router_expert[l],
                 w_gate[l], w_up[l], w_down[l])
        conv0 = jnp.zeros((bp, CONV_W - 1, QKV_A), xp.dtype)
        s0 = jnp.zeros((bp, H_A, DK_A, DV_A), xp.dtype)
        y_mix, k_rows, v_rows, s_new, c_new = token_mixer(xp, pos_p, conv0, s0, None, None, lam, lam_init, *mix_w)
        xp = layer_norm(DN_ALPHA * xp + y_mix, ln1_g[l], ln1_b[l])
        xp = layer_norm(DN_ALPHA * xp + hierarchical_moe(xp, *moe_w), ln2_g[l], ln2_b[l])
        kp_l.append(k_rows); vp_l.append(v_rows); sp_l.append(s_new); cp_l.append(c_new)
        k_past = cache_k[l][page_table].reshape(bs, past_len, H_B, 2 * D_B)
        v_past = cache_v[l][page_table].reshape(bs, past_len, H_B, 2 * D_B)
        y_mix, k_rows, v_rows, s_new, c_new = token_mixer(xs, pos_s, state_conv[l], state_delta[l],
                                                          k_past, v_past, lam, lam_init, *mix_w)
        xs = layer_norm(DN_ALPHA * xs + y_mix, ln1_g[l], ln1_b[l])
        xs = layer_norm(DN_ALPHA * xs + hierarchical_moe(xs, *moe_w), ln2_g[l], ln2_b[l])
        ks_l.append(k_rows); vs_l.append(v_rows); ss_l.append(s_new); cs_l.append(c_new)
    return (xp, xs,
            jnp.stack(kp_l), jnp.stack(vp_l), jnp.stack(sp_l), jnp.stack(cp_l),
            jnp.stack(ks_l), jnp.stack(vs_l), jnp.stack(ss_l), jnp.stack(cs_l))
```

```python
import functools
import math

import jax
import jax.numpy as jnp
from jax import lax
from jax.experimental import pallas as pl
from jax.experimental.pallas import tpu as pltpu

F32 = jnp.float32
BF16 = jnp.bfloat16
HIGHEST = lax.Precision.HIGHEST

LANES = 128
SUBLANES = 8
VMEM_LIMIT_BYTES = 56 * 1024 * 1024

LN_EPS = 1e-5
RMS_EPS = 1e-6
ROPE_THETA = 500000.0
CONV_TAIL = SUBLANES
GDN_CHUNK = 64
NEG = -1e30


def _pick_tile(n, preferred):
    if n <= preferred:
        return n
    for t in range(preferred, 0, -1):
        if n % t == 0 and t % SUBLANES == 0:
            return t
    return n


def _params(*semantics):
    return pltpu.CompilerParams(dimension_semantics=semantics, vmem_limit_bytes=VMEM_LIMIT_BYTES)


def _dot(a, b, precision=None):
    return jnp.dot(a, b, preferred_element_type=F32, precision=precision)


def _dot_nt(a, b, precision=None):
    return lax.dot_general(a, b, (((1,), (1,)), ((), ())), preferred_element_type=F32, precision=precision)


def _dot_tn(a, b, precision=None):
    return lax.dot_general(a, b, (((0,), (0,)), ((), ())), preferred_element_type=F32, precision=precision)


def _layer_norm(x, g, b):
    mu = jnp.mean(x, axis=-1, keepdims=True)
    xc = x - mu
    var = jnp.mean(xc * xc, axis=-1, keepdims=True)
    return xc * lax.rsqrt(var + LN_EPS) * g + b


def _rms_norm(x, w):
    return x * lax.rsqrt(jnp.mean(x * x, axis=-1, keepdims=True) + RMS_EPS) * w


def _l2_normalize(x):
    return x * lax.rsqrt(jnp.sum(x * x, axis=-1, keepdims=True) + RMS_EPS)


def _silu(x):
    return x * jax.nn.sigmoid(x)


def _softplus(x):
    return jnp.maximum(x, 0.0) + jnp.log1p(jnp.exp(-jnp.abs(x)))


def _lambda(lamv, lam_init):
    s1 = jnp.sum(lamv[0:1] * lamv[1:2], axis=-1, keepdims=True)
    s2 = jnp.sum(lamv[2:3] * lamv[3:4], axis=-1, keepdims=True)
    return jnp.exp(s1) - jnp.exp(s2) + lam_init


def _in_proj_kernel(x_ref, g_ref, b_ref, w_ref, cos_ref, sa_ref, sb_ref,
                    qkv_ref, z_ref, ba_ref, q16_ref, k32_ref, k16_ref, v32_ref, v16_ref,
                    *, apply_ln, c_qkv, c_z, c_b, q_scale, rope_half):
    x = x_ref[...]
    if apply_ln:
        x = _layer_norm(x, g_ref[...], b_ref[...])
    xb = x.astype(BF16)

    def proj(lo, width):
        return _dot(xb, w_ref[:, lo:lo + width])

    qkv_ref[...] = proj(0, c_qkv)
    z_ref[...] = proj(c_qkv, c_z)
    base = c_qkv + c_z
    q = proj(base, c_b)
    k = proj(base + c_b, c_b)
    v = proj(base + 2 * c_b, c_b)
    ba_ref[...] = proj(base + 3 * c_b, LANES)

    cos = cos_ref[...]
    sa = sa_ref[...]
    sb = sb_ref[...]

    def rope(y):
        outs = []
        for h in range(c_b // LANES):
            yh = y[:, h * LANES:(h + 1) * LANES]
            up = pltpu.roll(yh, LANES - rope_half, 1)
            dn = pltpu.roll(yh, rope_half, 1)
            outs.append(yh * cos + up * sa + dn * sb)
        return jnp.concatenate(outs, axis=-1)

    q = rope(q) * q_scale
    k = rope(k)
    q16_ref[...] = q.astype(BF16)
    k32_ref[...] = k
    k16_ref[...] = k.astype(BF16)
    v32_ref[...] = v
    v16_ref[...] = v.astype(BF16)


def _in_proj(x2d, ln_g, ln_b, w_bf16, tabs, *, apply_ln, c_qkv, c_z, c_b, q_scale, rope_half):
    n, d = x2d.shape
    cos, sa, sb = tabs
    tm = _pick_tile(math.gcd(n, cos.shape[0]), 512)
    tab_tiles = cos.shape[0] // tm
    row = lambda i: (i, 0)
    fixed = lambda i: (0, 0)
    tab = lambda i: (i % tab_tiles, 0)
    outs = [
        jax.ShapeDtypeStruct((n, c_qkv), F32), jax.ShapeDtypeStruct((n, c_z), F32),
        jax.ShapeDtypeStruct((n, LANES), F32), jax.ShapeDtypeStruct((n, c_b), BF16),
        jax.ShapeDtypeStruct((n, c_b), F32), jax.ShapeDtypeStruct((n, c_b), BF16),
        jax.ShapeDtypeStruct((n, c_b), F32), jax.ShapeDtypeStruct((n, c_b), BF16),
    ]
    return pl.pallas_call(
        functools.partial(_in_proj_kernel, apply_ln=apply_ln, c_qkv=c_qkv, c_z=c_z, c_b=c_b,
                          q_scale=q_scale, rope_half=rope_half),
        out_shape=outs,
        grid=(n // tm,),
        in_specs=[
            pl.BlockSpec((tm, d), row), pl.BlockSpec((1, d), fixed), pl.BlockSpec((1, d), fixed),
            pl.BlockSpec(w_bf16.shape, fixed),
            pl.BlockSpec((tm, LANES), tab), pl.BlockSpec((tm, LANES), tab), pl.BlockSpec((tm, LANES), tab),
        ],
        out_specs=[pl.BlockSpec((tm, o.shape[1]), row) for o in outs],
        compiler_params=_params("parallel"),
        name="in_proj",
    )(x2d, ln_g, ln_b, w_bf16, cos, sa, sb)


def _rope_tables(pos, d_map, rope_dim):
    half = rope_dim // 2
    inv = ROPE_THETA ** (-jnp.arange(0, rope_dim, 2, dtype=F32) / rope_dim)
    ang = pos.astype(F32)[:, None] * inv
    cos, sin = jnp.cos(ang), jnp.sin(ang)
    n = pos.shape[0]
    ones = jnp.ones((n, d_map - rope_dim), F32)
    zeros_rest = jnp.zeros((n, d_map - rope_dim), F32)
    zeros_half = jnp.zeros((n, half), F32)
    cos_m = jnp.concatenate([cos, cos, ones], axis=1)
    sa_m = jnp.concatenate([-sin, zeros_half, zeros_rest], axis=1)
    sb_m = jnp.concatenate([zeros_half, sin, zeros_rest], axis=1)
    return tuple(jnp.concatenate([t, t], axis=1) for t in (cos_m, sa_m, sb_m))


def _neumann_series(m, precision):
    n = m.shape[0]
    acc = m
    q = _dot(m, m, precision)
    span = 4
    while span < n:
        r = _dot(jnp.concatenate([acc, q], axis=0), q, precision)
        acc = acc + q + r[:n]
        q = r[n:]
        span *= 2
    return acc + q + _dot(acc, q, precision)


def _gdn_prompt_kernel(qkv_ref, z_ref, ba_ref, cw_ref, alog_ref, dt_ref, nw_ref,
                       o_ref, s_ref, y_scr, g_scr, beta_scr, prev_scr, state_scr,
                       *, n_heads, dk, dv, chunk):
    t = pl.program_id(1)
    tt, c = qkv_ref.shape

    @pl.when(t == 0)
    def _():
        prev_scr[...] = jnp.zeros_like(prev_scr)
        state_scr[...] = jnp.zeros_like(state_scr)

    x = qkv_ref[...]
    prev = prev_scr[...]
    cw = cw_ref[...]
    kw = cw.shape[0]
    tail_row = lax.broadcasted_iota(jnp.int32, (CONV_TAIL, c), 0)
    acc = x * cw[kw - 1:kw, :]
    for j in range(1, kw):
        xs = pltpu.roll(x, j, 0)
        ps = pltpu.roll(prev, j, 0)
        head = jnp.where(tail_row < j, ps, xs[0:CONV_TAIL])
        xs = jnp.concatenate([head, xs[CONV_TAIL:]], axis=0)
        acc = acc + xs * cw[kw - 1 - j:kw - j, :]
    prev_scr[...] = x[tt - CONV_TAIL:tt]
    y_scr[...] = _silu(acc)

    ba = ba_ref[...]
    beta_scr[...] = jax.nn.sigmoid(ba)
    g_scr[...] = -jnp.exp(alog_ref[...]) * _softplus(ba + dt_ref[...])

    ri = lax.broadcasted_iota(jnp.int32, (chunk, chunk), 0)
    ci = lax.broadcasted_iota(jnp.int32, (chunk, chunk), 1)
    incl = ri >= ci
    strict = ri > ci
    ltri = incl.astype(F32)
    sel_r = lax.broadcasted_iota(jnp.int32, (SUBLANES, LANES), 0)
    sel_c = lax.broadcasted_iota(jnp.int32, (SUBLANES, LANES), 1)
    head_rows = (sel_c == sel_r + n_heads).astype(F32)
    nw = nw_ref[...]

    def chunk_body(ic, carry):
        r0 = pl.multiple_of(ic * chunk, chunk)
        g_c = g_scr[pl.ds(r0, chunk), :]
        b_c = beta_scr[pl.ds(r0, chunk), :]
        y_c = y_scr[pl.ds(r0, chunk), :]
        z_c = z_ref[pl.ds(r0, chunk), :]
        gam = _dot(ltri, g_c, HIGHEST)
        gam_rows = _dot_nt(head_rows, gam, HIGHEST)
        e_gam = jnp.exp(gam)
        g_last = gam[chunk - 1:chunk, :]
        k_end_scale = jnp.exp(g_last - gam)
        g_end = jnp.exp(g_last)
        outs = []
        for h in range(n_heads):
            gl = n_heads + h
            qh = _l2_normalize(y_c[:, h * dk:(h + 1) * dk]) * (dk ** -0.5)
            kh = _l2_normalize(y_c[:, (n_heads + h) * dk:(n_heads + h + 1) * dk])
            vh = y_c[:, 2 * n_heads * dk + h * dv:2 * n_heads * dk + (h + 1) * dv]
            beta_h = b_c[:, h:h + 1]
            diff = gam[:, gl:gl + 1] - gam_rows[h:h + 1, :]
            decay = jnp.where(incl, jnp.exp(jnp.where(incl, diff, 0.0)), 0.0)
            kb = kh.astype(BF16)
            qk_kk = _dot_nt(jnp.concatenate([qh, kh], axis=0).astype(BF16), kb)
            attn = qk_kk[:chunk] * decay
            m = jnp.where(strict, -(beta_h * qk_kk[chunk:] * decay), 0.0)
            inv_minus_i = _neumann_series(m, HIGHEST)
            eg_h = e_gam[:, gl:gl + 1]
            rhs = jnp.concatenate([beta_h * vh, (beta_h * eg_h) * kh], axis=-1)
            sol = rhs + _dot(inv_minus_i.astype(BF16), rhs.astype(BF16))
            w_v = sol[:, :dv]
            w_k = sol[:, dv:]
            q_dec = eg_h * qh
            k_end = k_end_scale[:, gl:gl + 1] * kh
            s = state_scr[h]
            wq = _dot(jnp.concatenate([w_k, q_dec], axis=0).astype(BF16), s.astype(BF16))
            u = w_v - wq[:chunk]
            ub = u.astype(BF16)
            o = wq[chunk:] + _dot(attn.astype(BF16), ub)
            state_scr[h] = g_end[:, gl:gl + 1] * s + _dot_tn(k_end.astype(BF16), ub)
            outs.append(_rms_norm(o, nw) * _silu(z_c[:, h * dv:(h + 1) * dv]))
        o_ref[pl.ds(r0, chunk), :] = jnp.concatenate(outs, axis=-1)
        return carry

    lax.fori_loop(0, tt // chunk, chunk_body, 0)

    @pl.when(t == pl.num_programs(1) - 1)
    def _():
        s_ref[...] = state_scr[...]


def _gdn_prompt(qkv, z, ba, conv_w, alog_pad, dt_pad, norm_w, *, n_heads, dk, dv):
    b, t, c = qkv.shape
    tt = _pick_tile(t, 512)
    assert tt % GDN_CHUNK == 0 and tt >= CONV_TAIL
    tile = lambda ib, it: (ib, it, 0)
    fixed = lambda ib, it: (0, 0)
    return pl.pallas_call(
        functools.partial(_gdn_prompt_kernel, n_heads=n_heads, dk=dk, dv=dv, chunk=GDN_CHUNK),
        out_shape=[jax.ShapeDtypeStruct((b, t, n_heads * dv), F32),
                   jax.ShapeDtypeStruct((b, n_heads, dk, dv), F32)],
        grid=(b, t // tt),
        in_specs=[
            pl.BlockSpec((None, tt, c), tile), pl.BlockSpec((None, tt, n_heads * dv), tile),
            pl.BlockSpec((None, tt, LANES), tile),
            pl.BlockSpec(conv_w.shape, fixed), pl.BlockSpec((1, LANES), fixed), pl.BlockSpec((1, LANES), fixed),
            pl.BlockSpec((1, dv), fixed),
        ],
        out_specs=[pl.BlockSpec((None, tt, n_heads * dv), tile),
                   pl.BlockSpec((None, n_heads, dk, dv), lambda ib, it: (ib, 0, 0, 0))],
        scratch_shapes=[pltpu.VMEM((tt, c), F32), pltpu.VMEM((tt, LANES), F32), pltpu.VMEM((tt, LANES), F32),
                        pltpu.VMEM((CONV_TAIL, c), F32), pltpu.VMEM((n_heads, dk, dv), F32)],
        compiler_params=_params("parallel", "arbitrary"),
        name="gdn_prompt",
    )(qkv, z, ba, conv_w, alog_pad, dt_pad, norm_w)


def _gdn_step_kernel(qkv_ref, sc_ref, z_ref, ba_ref, cw_ref, alog_ref, dt_ref, nw_ref, s_ref,
                     o_ref, sn_ref, cn_ref, *, n_heads, dk, dv):
    nb, c = qkv_ref.shape
    x = qkv_ref[...]
    sc = sc_ref[...]
    cw = cw_ref[...]
    kw = cw.shape[0]
    acc = x * cw[kw - 1:kw, :]
    for i in range(kw - 1):
        acc = acc + sc[:, i * c:(i + 1) * c] * cw[i:i + 1, :]
    y = _silu(acc)
    cn_ref[...] = jnp.concatenate([sc[:, c:], x], axis=-1)

    ba = ba_ref[...]
    beta = jax.nn.sigmoid(ba)
    e_g = jnp.exp(-jnp.exp(alog_ref[...]) * _softplus(ba + dt_ref[...]))
    nw = nw_ref[...]
    pad_k = jnp.zeros((SUBLANES - 2, dk), F32)
    pad_1k = jnp.zeros((SUBLANES - 1, dk), F32)
    pad_1v = jnp.zeros((SUBLANES - 1, dv), F32)
    outs = []
    for h in range(n_heads):
        gl = n_heads + h
        qh = _l2_normalize(y[:, h * dk:(h + 1) * dk]) * (dk ** -0.5)
        kh = _l2_normalize(y[:, (n_heads + h) * dk:(n_heads + h + 1) * dk])
        vh = y[:, 2 * n_heads * dk + h * dv:2 * n_heads * dk + (h + 1) * dv]
        beta_h = beta[:, h:h + 1]
        eg_h = e_g[:, gl:gl + 1]
        qk = jnp.sum(qh * kh, axis=-1, keepdims=True)
        rows = []
        for ib in range(nb):
            s = s_ref[ib, h]
            k_row = kh[ib:ib + 1]
            lhs = jnp.concatenate([k_row, qh[ib:ib + 1], pad_k], axis=0)
            kq = _dot(lhs, s)
            b_i = beta_h[ib:ib + 1]
            e_i = eg_h[ib:ib + 1]
            u = b_i * vh[ib:ib + 1] - (b_i * e_i) * kq[0:1]
            rows.append(e_i * kq[1:2] + qk[ib:ib + 1] * u)
            outer = _dot_tn(jnp.concatenate([k_row, pad_1k], axis=0), jnp.concatenate([u, pad_1v], axis=0))
            sn_ref[ib, h] = e_i * s + outer
        o = jnp.concatenate(rows, axis=0)
        outs.append(_rms_norm(o, nw) * _silu(z_ref[:, h * dv:(h + 1) * dv]))
    o_ref[...] = jnp.concatenate(outs, axis=-1)


def _gdn_step(qkv, conv_state, z, ba, conv_w, alog_pad, dt_pad, norm_w, state, *, n_heads, dk, dv):
    n, c = qkv.shape
    nb = _pick_tile(n, SUBLANES)
    row = lambda i: (i, 0)
    fixed = lambda i: (0, 0)
    st = lambda i: (i, 0, 0, 0)
    return pl.pallas_call(
        functools.partial(_gdn_step_kernel, n_heads=n_heads, dk=dk, dv=dv),
        out_shape=[jax.ShapeDtypeStruct((n, n_heads * dv), F32), jax.ShapeDtypeStruct(state.shape, F32),
                   jax.ShapeDtypeStruct(conv_state.shape, F32)],
        grid=(n // nb,),
        in_specs=[
            pl.BlockSpec((nb, c), row), pl.BlockSpec((nb, conv_state.shape[1]), row),
            pl.BlockSpec((nb, n_heads * dv), row), pl.BlockSpec((nb, LANES), row),
            pl.BlockSpec(conv_w.shape, fixed), pl.BlockSpec((1, LANES), fixed), pl.BlockSpec((1, LANES), fixed),
            pl.BlockSpec((1, dv), fixed), pl.BlockSpec((nb, n_heads, dk, dv), st),
        ],
        out_specs=[pl.BlockSpec((nb, n_heads * dv), row), pl.BlockSpec((nb, n_heads, dk, dv), st),
                   pl.BlockSpec((nb, conv_state.shape[1]), row)],
        compiler_params=_params("parallel"),
        name="gdn_step",
    )(qkv, conv_state, z, ba, conv_w, alog_pad, dt_pad, norm_w, state)


def _softmax_step(s, v_bf16, m, l, acc):
    m_new = jnp.maximum(m, jnp.max(s, axis=-1, keepdims=True))
    alpha = jnp.exp(m - m_new)
    p = jnp.exp(s - m_new)
    l = alpha * l + jnp.sum(p, axis=-1, keepdims=True)
    acc = alpha * acc + _dot(p.astype(BF16), v_bf16)
    return m_new, l, acc


def _attn_prompt_kernel(lamv_ref, nw_ref, q_ref, k_ref, v_ref, o_ref, *, tk, d_map, lam_init):
    iq = pl.program_id(2)
    tq, dh = q_ref.shape
    q = q_ref[...]
    lane = lax.broadcasted_iota(jnp.int32, (tq, dh), 1)
    zero = jnp.zeros_like(q)
    q2 = jnp.concatenate([jnp.where(lane < d_map, q, zero), jnp.where(lane >= d_map, q, zero)], axis=0)
    rows = 2 * tq

    def full_block(j, carry):
        off = pl.multiple_of(j * tk, tk)
        s = _dot_nt(q2, k_ref[pl.ds(off, tk), :])
        return _softmax_step(s, v_ref[pl.ds(off, tk), :], *carry)

    carry = (jnp.full((rows, 1), NEG, F32), jnp.zeros((rows, 1), F32), jnp.zeros((rows, dh), F32))
    n_full = (iq * tq) // tk
    carry = lax.fori_loop(0, n_full, full_block, carry)

    r = lax.broadcasted_iota(jnp.int32, (rows, tk), 0)
    col_minus_row = lax.broadcasted_iota(jnp.int32, (rows, tk), 1) - jnp.where(r < tq, r, r - tq)
    for jj in range(max(1, tq // tk)):
        j = n_full + jj
        off = pl.multiple_of(j * tk, tk)
        s = _dot_nt(q2, k_ref[pl.ds(off, tk), :])
        s = jnp.where(col_minus_row <= iq * tq - j * tk, s, NEG)
        carry = _softmax_step(s, v_ref[pl.ds(off, tk), :], *carry)

    _, l, acc = carry
    on = acc / l
    lam = _lambda(lamv_ref[...], lam_init)
    o = on[:tq] - lam * on[tq:]
    o_ref[...] = _rms_norm(o, nw_ref[...]) * (1.0 - lam_init)


def _attn_prompt(q16, k16, v16, lamv, norm_w, *, n_heads, d_map, lam_init):
    b, t, _ = q16.shape
    dh = 2 * d_map
    tq = _pick_tile(t, 256)
    tk = _pick_tile(t, 512)
    assert tq % tk == 0 or tk % tq == 0
    fixed = lambda ib, ih, iq: (0, 0)
    return pl.pallas_call(
        functools.partial(_attn_prompt_kernel, tk=tk, d_map=d_map, lam_init=lam_init),
        out_shape=jax.ShapeDtypeStruct((b, t, n_heads * dh), F32),
        grid=(b, n_heads, t // tq),
        in_specs=[
            pl.BlockSpec(lamv.shape, fixed), pl.BlockSpec((1, dh), fixed),
            pl.BlockSpec((None, tq, dh), lambda ib, ih, iq: (ib, iq, ih)),
            pl.BlockSpec((None, t, dh), lambda ib, ih, iq: (ib, 0, ih)),
            pl.BlockSpec((None, t, dh), lambda ib, ih, iq: (ib, 0, ih)),
        ],
        out_specs=pl.BlockSpec((None, tq, dh), lambda ib, ih, iq: (ib, iq, ih)),
        compiler_params=_params("parallel", "parallel", "arbitrary"),
        name="attn_prompt",
    )(lamv, norm_w, q16, k16, v16)


def _attn_decode_kernel(pt_ref, lamv_ref, nw_ref, q_ref, kn_ref, vn_ref, *rest,
                        pages, n_heads, d_map, lam_init):
    del pt_ref
    k_refs = rest[:pages]
    v_refs = rest[pages:2 * pages]
    o_ref = rest[2 * pages]
    qmat_scr, m_scr, l_scr, acc_scr = rest[2 * pages + 1:]
    ic = pl.program_id(1)
    dh = 2 * d_map
    rows = qmat_scr.shape[0]

    def head_rows(x, mask_maps):
        lane = lax.broadcasted_iota(jnp.int32, (1, dh), 1)
        out = []
        for m in range(2):
            for h in range(n_heads):
                xh = x[:, h * dh:(h + 1) * dh]
                if mask_maps:
                    xh = jnp.where((lane >= m * d_map) & (lane < (m + 1) * d_map), xh, 0.0)
                out.append(xh)
        out.append(jnp.zeros((rows - 2 * n_heads, dh), F32))
        return jnp.concatenate(out, axis=0)

    @pl.when(ic == 0)
    def _():
        qmat_scr[...] = head_rows(q_ref[...].astype(F32), True)
        m_scr[...] = jnp.full_like(m_scr, NEG)
        l_scr[...] = jnp.zeros_like(l_scr)
        acc_scr[...] = jnp.zeros_like(acc_scr)

    k_cat = jnp.concatenate([k[...] for k in k_refs], axis=0).astype(BF16)
    v_cat = jnp.concatenate([v[...] for v in v_refs], axis=0).astype(BF16)
    s = _dot_nt(qmat_scr[...].astype(BF16), k_cat)
    r = lax.broadcasted_iota(jnp.int32, s.shape, 0)
    col = lax.broadcasted_iota(jnp.int32, s.shape, 1)
    s = jnp.where(col % n_heads == r % n_heads, s, NEG)
    m_new, l_new, acc_new = _softmax_step(s, v_cat, m_scr[...], l_scr[...], acc_scr[...])
    m_scr[...] = m_new
    l_scr[...] = l_new
    acc_scr[...] = acc_new

    @pl.when(ic == pl.num_programs(1) - 1)
    def _():
        s_new = jnp.sum(qmat_scr[...] * head_rows(kn_ref[...], False), axis=-1, keepdims=True)
        m_f = jnp.maximum(m_new, s_new)
        a = jnp.exp(m_new - m_f)
        p_new = jnp.exp(s_new - m_f)
        l_f = a * l_new + p_new
        acc_f = a * acc_new + p_new * head_rows(vn_ref[...], False)
        on = acc_f / l_f
        lam = _lambda(lamv_ref[...], lam_init)
        o = on[0:n_heads] - lam * on[n_heads:2 * n_heads]
        o = _rms_norm(o, nw_ref[...]) * (1.0 - lam_init)
        o_ref[...] = jnp.concatenate([o[h:h + 1] for h in range(n_heads)], axis=-1)


def _attn_decode(q16, k_new, v_new, cache_k, cache_v, page_table, layer, lamv, norm_w,
                 *, n_heads, d_map, lam_init):
    bs = q16.shape[0]
    dh = 2 * d_map
    depth, n_pool, page, _, _ = cache_k.shape
    n_pages = page_table.shape[1]
    pages = _pick_tile(n_pages, 8) if n_pages % 8 == 0 else 1
    page_rows = page * n_heads
    ck = cache_k.reshape(depth * n_pool, page_rows, dh)
    cv = cache_v.reshape(depth * n_pool, page_rows, dh)
    q3 = q16.reshape(bs, 1, n_heads * dh)
    kn3 = k_new.reshape(bs, 1, n_heads * dh)
    vn3 = v_new.reshape(bs, 1, n_heads * dh)
    fixed = lambda ib, ic, pt: (0, 0)
    seq = lambda ib, ic, pt: (ib, 0, 0)

    def page_map(p):
        return lambda ib, ic, pt: (layer * n_pool + pt[ib, ic * pages + p], 0, 0)

    page_specs = [pl.BlockSpec((None, page_rows, dh), page_map(p)) for p in range(pages)]
    rows = 2 * SUBLANES
    assert rows >= 2 * n_heads
    grid_spec = pltpu.PrefetchScalarGridSpec(
        num_scalar_prefetch=1,
        grid=(bs, n_pages // pages),
        in_specs=[pl.BlockSpec(lamv.shape, fixed), pl.BlockSpec((1, dh), fixed),
                  pl.BlockSpec((None, 1, n_heads * dh), seq), pl.BlockSpec((None, 1, n_heads * dh), seq),
                  pl.BlockSpec((None, 1, n_heads * dh), seq)] + page_specs + page_specs,
        out_specs=pl.BlockSpec((None, 1, n_heads * dh), seq),
        scratch_shapes=[pltpu.VMEM((rows, dh), F32), pltpu.VMEM((rows, 1), F32), pltpu.VMEM((rows, 1), F32),
                        pltpu.VMEM((rows, dh), F32)],
    )
    out = pl.pallas_call(
        functools.partial(_attn_decode_kernel, pages=pages, n_heads=n_heads, d_map=d_map, lam_init=lam_init),
        out_shape=jax.ShapeDtypeStruct((bs, 1, n_heads * dh), F32),
        grid_spec=grid_spec,
        compiler_params=_params("parallel", "arbitrary"),
        name="attn_decode",
    )(page_table, lamv, norm_w, q3, kn3, vn3, *([ck] * pages), *([cv] * pages))
    return out.reshape(bs, n_heads * dh)


def _route(logits, n_groups, n_exp):
    lane = lax.broadcasted_iota(jnp.int32, logits.shape, 1)
    ne = n_groups * n_exp
    big = jnp.int32(LANES)

    def first_max(mask):
        val = jnp.max(jnp.where(mask, logits, NEG), axis=-1, keepdims=True)
        idx = jnp.min(jnp.where(mask & (logits == val), lane, big), axis=-1, keepdims=True)
        return val, idx

    is_group = (lane >= ne) & (lane < ne + n_groups)
    g_max, g_lane = first_max(is_group)
    g_sum = jnp.sum(jnp.where(is_group, jnp.exp(jnp.where(is_group, logits, g_max) - g_max), 0.0),
                    axis=-1, keepdims=True)
    g_w = 1.0 / g_sum
    lo = (g_lane - ne) * n_exp
    in_group = (lane >= lo) & (lane < lo + n_exp)
    v1, i1 = first_max(in_group)
    v2, i2 = first_max(in_group & (lane != i1))
    e = jnp.exp(v2 - v1)
    w1 = g_w / (1.0 + e)
    w2 = g_w * e / (1.0 + e)
    return jnp.where(lane == i1, w1, jnp.where(lane == i2, w2, 0.0))


def _out_proj_kernel(oa_ref, ob_ref, x_ref, ing_ref, inb_ref, w_ref, g_ref, b_ref, wr_ref, br_ref,
                     x1_ref, x1b_ref, gate_ref, *, apply_ln, alpha, n_groups, n_exp):
    x = x_ref[...]
    if apply_ln:
        x = _layer_norm(x, ing_ref[...], inb_ref[...])
    wa = oa_ref.shape[1]
    y = _dot(oa_ref[...].astype(BF16), w_ref[:wa, :]) + _dot(ob_ref[...].astype(BF16), w_ref[wa:, :])
    x1 = _layer_norm(alpha * x + y, g_ref[...], b_ref[...])
    x1_ref[...] = x1
    x1b_ref[...] = x1.astype(BF16)
    logits = _dot(x1, wr_ref[...], HIGHEST) + br_ref[...]
    gate_ref[...] = _route(logits, n_groups, n_exp)


def _out_proj(o_a, o_b, x2d, in_g, in_b, w_bf16, g, b, w_router, b_router, *, apply_ln, alpha, n_groups, n_exp):
    n, d = x2d.shape
    tm = _pick_tile(n, 512)
    row = lambda i: (i, 0)
    fixed = lambda i: (0, 0)
    return pl.pallas_call(
        functools.partial(_out_proj_kernel, apply_ln=apply_ln, alpha=alpha, n_groups=n_groups, n_exp=n_exp),
        out_shape=[jax.ShapeDtypeStruct((n, d), F32), jax.ShapeDtypeStruct((n, d), BF16),
                   jax.ShapeDtypeStruct((n, LANES), F32)],
        grid=(n // tm,),
        in_specs=[
            pl.BlockSpec((tm, o_a.shape[1]), row), pl.BlockSpec((tm, o_b.shape[1]), row), pl.BlockSpec((tm, d), row),
            pl.BlockSpec((1, d), fixed), pl.BlockSpec((1, d), fixed), pl.BlockSpec(w_bf16.shape, fixed),
            pl.BlockSpec((1, d), fixed), pl.BlockSpec((1, d), fixed),
            pl.BlockSpec(w_router.shape, fixed), pl.BlockSpec((1, LANES), fixed),
        ],
        out_specs=[pl.BlockSpec((tm, d), row), pl.BlockSpec((tm, d), row), pl.BlockSpec((tm, LANES), row)],
        compiler_params=_params("parallel"),
        name="out_proj",
    )(o_a, o_b, x2d, in_g, in_b, w_bf16, g, b, w_router, b_router)


def _moe_kernel(xb_ref, x1_ref, gate_ref, wg_ref, wu_ref, wd_ref, g_ref, b_ref, o_ref, acc_ref, *, alpha):
    ie = pl.program_id(1)

    @pl.when(ie == 0)
    def _():
        acc_ref[...] = jnp.zeros_like(acc_ref)

    xb = xb_ref[...]
    gates = gate_ref[...]
    lane = lax.broadcasted_iota(jnp.int32, gates.shape, 1)
    gate_e = jnp.sum(jnp.where(lane == ie, gates, 0.0), axis=-1, keepdims=True)
    h = _silu(_dot(xb, wg_ref[...])) * _dot(xb, wu_ref[...])
    acc_ref[...] += _dot((h * gate_e).astype(BF16), wd_ref[...])

    @pl.when(ie == pl.num_programs(1) - 1)
    def _():
        o_ref[...] = _layer_norm(alpha * x1_ref[...] + acc_ref[...], g_ref[...], b_ref[...])


def _moe(x1b, x1, gates, wg, wu, wd, g, b, *, alpha):
    n, d = x1.shape
    n_exp_total, _, f = wg.shape
    tm = _pick_tile(n, 1024)
    row = lambda i, e: (i, 0)
    fixed = lambda i, e: (0, 0)
    exp = lambda i, e: (e, 0, 0)
    return pl.pallas_call(
        functools.partial(_moe_kernel, alpha=alpha),
        out_shape=jax.ShapeDtypeStruct((n, d), F32),
        grid=(n // tm, n_exp_total),
        in_specs=[
            pl.BlockSpec((tm, d), row), pl.BlockSpec((tm, d), row), pl.BlockSpec((tm, LANES), row),
            pl.BlockSpec((None, d, f), exp), pl.BlockSpec((None, d, f), exp), pl.BlockSpec((None, f, d), exp),
            pl.BlockSpec((1, d), fixed), pl.BlockSpec((1, d), fixed),
        ],
        out_specs=pl.BlockSpec((tm, d), row),
        scratch_shapes=[pltpu.VMEM((tm, d), F32)],
        compiler_params=_params("parallel", "arbitrary"),
        name="moe",
    )(x1b, x1, gates, wg, wu, wd, g, b)


def _pad_lanes(v, offset=0):
    return jnp.zeros((1, LANES), F32).at[0, offset:offset + v.shape[0]].set(v.astype(F32))


def kernel(x_prompt, x_sample, cache_k, cache_v, state_delta, state_conv, page_table, ln_in_g, ln_in_b, w_in, conv_w, a_log, dt_bias, gdn_norm_w, lambda_q1, lambda_k1, lambda_q2, lambda_k2, diff_norm_w, w_out, ln1_g, ln1_b, w_router_group, b_router_group, w_router_expert, b_router_expert, w_gate, w_up, w_down, ln2_g, ln2_b):
    bp, tp, d_model = x_prompt.shape
    bs, ts, _ = x_sample.shape
    depth = w_in.shape[0]
    n_heads_a = a_log.shape[1]
    dv = gdn_norm_w.shape[1]
    c_qkv = conv_w.shape[2]
    kw = conv_w.shape[1]
    c_z = n_heads_a * dv
    dk = (c_qkv - c_z) // (2 * n_heads_a)
    n_heads_b = cache_k.shape[3]
    d_map = lambda_q1.shape[1]
    dh = 2 * d_map
    c_b = n_heads_b * dh
    rope_dim = d_map // 4
    n_groups, n_exp = w_gate.shape[1], w_gate.shape[2]
    d_expert = w_gate.shape[4]
    n_pages = page_table.shape[1]
    past_len = n_pages * cache_k.shape[2]
    alpha = (2.0 * depth) ** 0.25
    q_scale = d_map ** -0.5

    assert ts == 1, "one new token per sample sequence"
    assert dh == LANES and dk == LANES and dv == LANES, "heads are one vreg wide"
    assert math.frexp(q_scale)[0] == 0.5, "score scale must be a power of two to fold into q exactly"
    assert 2 * n_heads_a <= LANES and n_groups * n_exp + n_groups <= LANES
    assert w_in.shape[2] == c_qkv + c_z + 2 * n_heads_a + 3 * c_b

    tabs_p = _rope_tables(jnp.arange(tp), d_map, rope_dim)
    tabs_s = tuple(jnp.tile(t, (bs, 1)) for t in _rope_tables(past_len + jnp.arange(ts), d_map, rope_dim))
    in_g, in_b = ln_in_g.reshape(1, d_model), ln_in_b.reshape(1, d_model)

    xp = x_prompt.reshape(bp * tp, d_model)
    xs = x_sample.reshape(bs * ts, d_model)
    outs = [[] for _ in range(8)]
    for l in range(depth):
        lam_init = 0.8 - 0.6 * math.exp(-0.3 * l)
        first = l == 0
        w = w_in[l]
        o_ba = c_qkv + c_z
        o_b = o_ba + 2 * n_heads_a
        w_ba = jnp.pad(w[:, o_ba:o_b], ((0, 0), (0, LANES - 2 * n_heads_a)))
        w_in_l = jnp.concatenate([w[:, :o_ba], w[:, o_b:], w_ba], axis=1).astype(BF16)
        w_out_l = w_out[l].astype(BF16)
        ne = n_groups * n_exp
        w_r = jnp.pad(jnp.concatenate([w_router_expert[l], w_router_group[l]], axis=1),
                      ((0, 0), (0, LANES - ne - n_groups)))
        b_r = _pad_lanes(jnp.concatenate([b_router_expert[l], b_router_group[l]]))
        wg = w_gate[l].reshape(ne, d_model, d_expert).astype(BF16)
        wu = w_up[l].reshape(ne, d_model, d_expert).astype(BF16)
        wd = w_down[l].reshape(ne, d_expert, d_model).astype(BF16)
        alog_pad = _pad_lanes(a_log[l], n_heads_a)
        dt_pad = _pad_lanes(dt_bias[l], n_heads_a)
        gdn_w = gdn_norm_w[l].reshape(1, dv)
        diff_w = diff_norm_w[l].reshape(1, dh)
        lamv = jnp.concatenate([_pad_lanes(v) for v in (lambda_q1[l], lambda_k1[l], lambda_q2[l], lambda_k2[l])]
                               + [jnp.zeros((SUBLANES - 4, LANES), F32)], axis=0)
        ln1 = (ln1_g[l].reshape(1, d_model), ln1_b[l].reshape(1, d_model))
        ln2 = (ln2_g[l].reshape(1, d_model), ln2_b[l].reshape(1, d_model))
        proj = functools.partial(_in_proj, apply_ln=first, c_qkv=c_qkv, c_z=c_z, c_b=c_b, q_scale=q_scale,
                                 rope_half=rope_dim // 2)
        mix_out = functools.partial(_out_proj, apply_ln=first, alpha=alpha, n_groups=n_groups, n_exp=n_exp)
        heads_a = dict(n_heads=n_heads_a, dk=dk, dv=dv)
        heads_b = dict(n_heads=n_heads_b, d_map=d_map, lam_init=lam_init)

        qkv, z, ba, q16, k32, k16, v32, v16 = proj(xp, in_g, in_b, w_in_l, tabs_p)
        shape3 = lambda a: a.reshape(bp, tp, a.shape[1])
        o_a, s_new = _gdn_prompt(shape3(qkv), shape3(z), shape3(ba), conv_w[l], alog_pad, dt_pad, gdn_w, **heads_a)
        o_bp = _attn_prompt(shape3(q16), shape3(k16), shape3(v16), lamv, diff_w, **heads_b)
        x1, x1b, gates = mix_out(o_a.reshape(bp * tp, c_z), o_bp.reshape(bp * tp, c_b), xp, in_g, in_b,
                                 w_out_l, *ln1, w_r, b_r)
        conv_rows = shape3(qkv)[:, tp - (kw - 1):, :]
        if tp < kw - 1:
            conv_rows = jnp.pad(shape3(qkv), ((0, 0), (kw - 1 - tp, 0), (0, 0)))
        xp = _moe(x1b, x1, gates, wg, wu, wd, *ln2, alpha=alpha)
        for i, a in enumerate((k32.reshape(bp, tp, n_heads_b, dh), v32.reshape(bp, tp, n_heads_b, dh),
                               s_new, conv_rows)):
            outs[i].append(a)

        qkv, z, ba, q16, k32, _, v32, _ = proj(xs, in_g, in_b, w_in_l, tabs_s)
        o_a, s_new, c_new = _gdn_step(qkv, state_conv[l].reshape(bs, (kw - 1) * c_qkv), z, ba, conv_w[l],
                                      alog_pad, dt_pad, gdn_w, state_delta[l], **heads_a)
        o_bs = _attn_decode(q16, k32, v32, cache_k, cache_v, page_table, l, lamv, diff_w, **heads_b)
        x1, x1b, gates = mix_out(o_a, o_bs, xs, in_g, in_b, w_out_l, *ln1, w_r, b_r)
        xs = _moe(x1b, x1, gates, wg, wu, wd, *ln2, alpha=alpha)
        for i, a in enumerate((k32.reshape(bs, ts, n_heads_b, dh), v32.reshape(bs, ts, n_heads_b, dh),
                               s_new, c_new.reshape(bs, kw - 1, c_qkv))):
            outs[4 + i].append(a)

    return (xp.reshape(bp, tp, d_model), xs.reshape(bs, ts, d_model)) + tuple(jnp.stack(o) for o in outs)
```

```python
import functools
import math

import jax
import jax.numpy as jnp
from jax import lax
from jax.experimental import pallas as pl
from jax.experimental.pallas import tpu as pltpu

F32 = jnp.float32
BF16 = jnp.bfloat16
HIGHEST = lax.Precision.HIGHEST

LANES = 128
SUBLANES = 8
VMEM_LIMIT_BYTES = 56 * 1024 * 1024

LN_EPS = 1e-5
RMS_EPS = 1e-6
ROPE_THETA = 500000.0
CONV_TAIL = SUBLANES
GDN_CHUNK = 64
GDN_GROUP = 4
ATTN_GROUP = 4
NEG = -1e30


def _pick_tile(n, preferred):
    if n <= preferred:
        return n
    for t in range(preferred, 0, -1):
        if n % t == 0 and t % SUBLANES == 0:
            return t
    return n


def _params(*semantics):
    return pltpu.CompilerParams(dimension_semantics=semantics, vmem_limit_bytes=VMEM_LIMIT_BYTES)


def _dot(a, b, precision=None):
    return jnp.dot(a, b, preferred_element_type=F32, precision=precision)


def _dot_nt(a, b, precision=None):
    return lax.dot_general(a, b, (((1,), (1,)), ((), ())), preferred_element_type=F32, precision=precision)


def _dot_tn(a, b, precision=None):
    return lax.dot_general(a, b, (((0,), (0,)), ((), ())), preferred_element_type=F32, precision=precision)


def _layer_norm(x, g, b):
    mu = jnp.mean(x, axis=-1, keepdims=True)
    xc = x - mu
    var = jnp.mean(xc * xc, axis=-1, keepdims=True)
    return xc * lax.rsqrt(var + LN_EPS) * g + b


def _rms_norm(x, w):
    return x * lax.rsqrt(jnp.mean(x * x, axis=-1, keepdims=True) + RMS_EPS) * w


def _l2_normalize(x):
    return x * lax.rsqrt(jnp.sum(x * x, axis=-1, keepdims=True) + RMS_EPS)


def _silu(x):
    return x * jax.nn.sigmoid(x)


def _softplus(x):
    return jnp.maximum(x, 0.0) + jnp.log1p(jnp.exp(-jnp.abs(x)))


def _lambda(lamv, lam_init):
    s1 = jnp.sum(lamv[0:1] * lamv[1:2], axis=-1, keepdims=True)
    s2 = jnp.sum(lamv[2:3] * lamv[3:4], axis=-1, keepdims=True)
    return jnp.exp(s1) - jnp.exp(s2) + lam_init


def _in_proj_kernel(x_ref, g_ref, b_ref, w_ref, cos_ref, sa_ref, sb_ref,
                    qkv_ref, z_ref, ba_ref, q16_ref, k32_ref, k16_ref, v32_ref, v16_ref,
                    *, apply_ln, c_qkv, c_z, c_b, q_scale, rope_half):
    x = x_ref[...]
    if apply_ln:
        x = _layer_norm(x, g_ref[...], b_ref[...])
    precise = w_ref.dtype == F32
    xin = x if precise else x.astype(BF16)

    def proj(lo, width):
        return _dot(xin, w_ref[:, lo:lo + width], HIGHEST if precise else None)

    qkv_ref[...] = proj(0, c_qkv)
    z_ref[...] = proj(c_qkv, c_z)
    base = c_qkv + c_z
    q = proj(base, c_b)
    k = proj(base + c_b, c_b)
    v = proj(base + 2 * c_b, c_b)
    ba_ref[...] = proj(base + 3 * c_b, LANES)

    cos = cos_ref[...]
    sa = sa_ref[...]
    sb = sb_ref[...]

    def rope(y):
        outs = []
        for h in range(c_b // LANES):
            yh = y[:, h * LANES:(h + 1) * LANES]
            up = pltpu.roll(yh, LANES - rope_half, 1)
            dn = pltpu.roll(yh, rope_half, 1)
            outs.append(yh * cos + up * sa + dn * sb)
        return jnp.concatenate(outs, axis=-1)

    q = rope(q) * q_scale
    k = rope(k)
    q16_ref[...] = q.astype(q16_ref.dtype)
    k16_ref[...] = k.astype(BF16)
    v16_ref[...] = v.astype(BF16)
    tm = x_ref.shape[0]
    n_b = c_b // LANES
    for h in range(n_b):
        k32_ref[pl.ds(h, tm, stride=n_b), :] = k[:, h * LANES:(h + 1) * LANES]
        v32_ref[pl.ds(h, tm, stride=n_b), :] = v[:, h * LANES:(h + 1) * LANES]


def _in_proj(x2d, ln_g, ln_b, w_bf16, tabs, *, apply_ln, c_qkv, c_z, c_b, q_scale, rope_half):
    n, d = x2d.shape
    cos, sa, sb = tabs
    tm = _pick_tile(math.gcd(n, cos.shape[0]), 512)
    tab_tiles = cos.shape[0] // tm
    row = lambda i: (i, 0)
    fixed = lambda i: (0, 0)
    tab = lambda i: (i % tab_tiles, 0)
    n_b = c_b // LANES
    outs = [
        jax.ShapeDtypeStruct((n, c_qkv), F32), jax.ShapeDtypeStruct((n, c_z), F32),
        jax.ShapeDtypeStruct((n, LANES), F32), jax.ShapeDtypeStruct((n, c_b), w_bf16.dtype),
        jax.ShapeDtypeStruct((n * n_b, LANES), F32), jax.ShapeDtypeStruct((n, c_b), BF16),
        jax.ShapeDtypeStruct((n * n_b, LANES), F32), jax.ShapeDtypeStruct((n, c_b), BF16),
    ]
    return pl.pallas_call(
        functools.partial(_in_proj_kernel, apply_ln=apply_ln, c_qkv=c_qkv, c_z=c_z, c_b=c_b,
                          q_scale=q_scale, rope_half=rope_half),
        out_shape=outs,
        grid=(n // tm,),
        in_specs=[
            pl.BlockSpec((tm, d), row), pl.BlockSpec((1, d), fixed), pl.BlockSpec((1, d), fixed),
            pl.BlockSpec(w_bf16.shape, fixed),
            pl.BlockSpec((tm, LANES), tab), pl.BlockSpec((tm, LANES), tab), pl.BlockSpec((tm, LANES), tab),
        ],
        out_specs=[pl.BlockSpec((tm * o.shape[0] // n, o.shape[1]), row) for o in outs],
        compiler_params=_params("parallel"),
        name="in_proj",
    )(x2d, ln_g, ln_b, w_bf16, cos, sa, sb)


def _rope_tables(pos, d_map, rope_dim):
    half = rope_dim // 2
    inv = ROPE_THETA ** (-jnp.arange(0, rope_dim, 2, dtype=F32) / rope_dim)
    ang = pos.astype(F32)[:, None] * inv
    cos, sin = jnp.cos(ang), jnp.sin(ang)
    n = pos.shape[0]
    ones = jnp.ones((n, d_map - rope_dim), F32)
    zeros_rest = jnp.zeros((n, d_map - rope_dim), F32)
    zeros_half = jnp.zeros((n, half), F32)
    cos_m = jnp.concatenate([cos, cos, ones], axis=1)
    sa_m = jnp.concatenate([-sin, zeros_half, zeros_rest], axis=1)
    sb_m = jnp.concatenate([zeros_half, sin, zeros_rest], axis=1)
    return tuple(jnp.concatenate([t, t], axis=1) for t in (cos_m, sa_m, sb_m))


def _split3(x):
    x1 = x.astype(BF16)
    r1 = x - x1.astype(F32)
    x2 = r1.astype(BF16)
    x3 = (r1 - x2.astype(F32)).astype(BF16)
    return x1, x2, x3


def _neumann_series(ms):
    n = ms[0].shape[0]
    accs = list(ms)
    qs = [_dot(m.astype(BF16), m.astype(BF16)) for m in ms]
    span = 4
    while span < n:
        rs = [_dot(jnp.concatenate([a.astype(BF16), q.astype(BF16)], axis=0), q.astype(BF16))
              for a, q in zip(accs, qs)]
        accs = [a + q + r[:n] for a, q, r in zip(accs, qs, rs)]
        qs = [r[n:] for r in rs]
        span *= 2
    return [a + q + _dot(a.astype(BF16), q.astype(BF16)) for a, q in zip(accs, qs)]


def _gdn_prompt_kernel(qkv_ref, z_ref, ba_ref, cw_ref, alog_ref, dt_ref, nw_ref,
                       o_ref, s_ref, y_scr, gam_scr, beta_scr, prev_scr, state_scr,
                       *, n_heads, dk, dv, chunk, group):
    t = pl.program_id(1)
    tt, c = qkv_ref.shape

    @pl.when(t == 0)
    def _():
        prev_scr[...] = jnp.zeros_like(prev_scr)
        state_scr[...] = jnp.zeros_like(state_scr)

    x = qkv_ref[...]
    prev = prev_scr[...]
    cw = cw_ref[...]
    kw = cw.shape[0]
    tail_row = lax.broadcasted_iota(jnp.int32, (CONV_TAIL, c), 0)
    acc = x * cw[kw - 1:kw, :]
    for j in range(1, kw):
        xs = pltpu.roll(x, j, 0)
        ps = pltpu.roll(prev, j, 0)
        head = jnp.where(tail_row < j, ps, xs[0:CONV_TAIL])
        xs = jnp.concatenate([head, xs[CONV_TAIL:]], axis=0)
        acc = acc + xs * cw[kw - 1 - j:kw - j, :]
    prev_scr[...] = x[tt - CONV_TAIL:tt]
    y_scr[...] = _silu(acc)

    ba = ba_ref[...]
    beta_scr[...] = jax.nn.sigmoid(ba)
    gam = -jnp.exp(alog_ref[...]) * _softplus(ba + dt_ref[...])
    row_in_chunk = lax.broadcasted_iota(jnp.int32, (tt, LANES), 0) % chunk
    step = 1
    while step < chunk:
        gam = gam + jnp.where(row_in_chunk >= step, pltpu.roll(gam, step, 0), 0.0)
        step *= 2
    gam_scr[...] = gam

    ri = lax.broadcasted_iota(jnp.int32, (chunk, chunk), 0)
    ci = lax.broadcasted_iota(jnp.int32, (chunk, chunk), 1)
    incl = ri >= ci
    strict = ri > ci
    sel_r = lax.broadcasted_iota(jnp.int32, (SUBLANES, LANES), 0)
    sel_c = lax.broadcasted_iota(jnp.int32, (SUBLANES, LANES), 1)
    head_rows = (sel_c == sel_r + n_heads).astype(BF16)
    nw = nw_ref[...]

    def chunk_group(ig, carry):
        starts = [pl.multiple_of((ig * group + c) * chunk, chunk) for c in range(group)]
        units = []
        for r0 in starts:
            gam = gam_scr[pl.ds(r0, chunk), :]
            b_c = beta_scr[pl.ds(r0, chunk), :]
            y_c = y_scr[pl.ds(r0, chunk), :]
            picked = _dot_nt(head_rows, jnp.concatenate(_split3(gam), axis=0))
            gam_rows = picked[:, :chunk] + picked[:, chunk:2 * chunk] + picked[:, 2 * chunk:]
            e_gam = jnp.exp(gam)
            g_last = gam[chunk - 1:chunk, :]
            k_end_scale = jnp.exp(g_last - gam)
            g_end = jnp.exp(g_last)
            for h in range(n_heads):
                gl = n_heads + h
                qh = _l2_normalize(y_c[:, h * dk:(h + 1) * dk]) * (dk ** -0.5)
                kh = _l2_normalize(y_c[:, (n_heads + h) * dk:(n_heads + h + 1) * dk])
                vh = y_c[:, 2 * n_heads * dk + h * dv:2 * n_heads * dk + (h + 1) * dv]
                diff = gam[:, gl:gl + 1] - gam_rows[h:h + 1, :]
                units.append(dict(
                    qh=qh, kh=kh, vh=vh, beta=b_c[:, h:h + 1], eg=e_gam[:, gl:gl + 1],
                    decay=jnp.where(incl, jnp.exp(jnp.where(incl, diff, 0.0)), 0.0),
                    k_end=k_end_scale[:, gl:gl + 1] * kh, g_end=g_end[:, gl:gl + 1]))
        qk_kk = [_dot_nt(jnp.concatenate([u["qh"], u["kh"]], axis=0).astype(BF16), u["kh"].astype(BF16))
                 for u in units]
        attn = [x[:chunk] * u["decay"] for x, u in zip(qk_kk, units)]
        ms = [jnp.where(strict, -(u["beta"] * x[chunk:] * u["decay"]), 0.0) for x, u in zip(qk_kk, units)]
        inv_minus_i = _neumann_series(ms)
        rhs = [jnp.concatenate([u["beta"] * u["vh"], (u["beta"] * u["eg"]) * u["kh"]], axis=-1) for u in units]
        sols = [r + _dot(n.astype(BF16), r.astype(BF16)) for n, r in zip(inv_minus_i, rhs)]

        states = [state_scr[h] for h in range(n_heads)]
        for c, r0 in enumerate(starts):
            ids = range(c * n_heads, (c + 1) * n_heads)
            wq = [_dot(jnp.concatenate([sols[i][:, dv:], units[i]["eg"] * units[i]["qh"]], axis=0).astype(BF16),
                       s.astype(BF16)) for i, s in zip(ids, states)]
            ub = [(sols[i][:, :dv] - x[:chunk]).astype(BF16) for i, x in zip(ids, wq)]
            o = [x[chunk:] + _dot(attn[i].astype(BF16), u) for i, x, u in zip(ids, wq, ub)]
            states = [units[i]["g_end"] * s + _dot_tn(units[i]["k_end"].astype(BF16), u)
                      for i, s, u in zip(ids, states, ub)]
            z_c = z_ref[pl.ds(r0, chunk), :]
            o_ref[pl.ds(r0, chunk), :] = jnp.concatenate(
                [_rms_norm(x, nw) * _silu(z_c[:, h * dv:(h + 1) * dv]) for h, x in enumerate(o)], axis=-1)
        for h in range(n_heads):
            state_scr[h] = states[h]
        return carry

    lax.fori_loop(0, tt // (group * chunk), chunk_group, 0)

    @pl.when(t == pl.num_programs(1) - 1)
    def _():
        s_ref[...] = state_scr[...]


def _gdn_prompt(qkv, z, ba, conv_w, alog_pad, dt_pad, norm_w, *, n_heads, dk, dv):
    b, t, c = qkv.shape
    tt = _pick_tile(t, 512)
    group = math.gcd(tt // GDN_CHUNK, GDN_GROUP)
    assert tt % (group * GDN_CHUNK) == 0 and tt >= CONV_TAIL
    tile = lambda ib, it: (ib, it, 0)
    fixed = lambda ib, it: (0, 0)
    return pl.pallas_call(
        functools.partial(_gdn_prompt_kernel, n_heads=n_heads, dk=dk, dv=dv, chunk=GDN_CHUNK, group=group),
        out_shape=[jax.ShapeDtypeStruct((b, t, n_heads * dv), F32),
                   jax.ShapeDtypeStruct((b, n_heads, dk, dv), F32)],
        grid=(b, t // tt),
        in_specs=[
            pl.BlockSpec((None, tt, c), tile), pl.BlockSpec((None, tt, n_heads * dv), tile),
            pl.BlockSpec((None, tt, LANES), tile),
            pl.BlockSpec(conv_w.shape, fixed), pl.BlockSpec((1, LANES), fixed), pl.BlockSpec((1, LANES), fixed),
            pl.BlockSpec((1, dv), fixed),
        ],
        out_specs=[pl.BlockSpec((None, tt, n_heads * dv), tile),
                   pl.BlockSpec((None, n_heads, dk, dv), lambda ib, it: (ib, 0, 0, 0))],
        scratch_shapes=[pltpu.VMEM((tt, c), F32), pltpu.VMEM((tt, LANES), F32), pltpu.VMEM((tt, LANES), F32),
                        pltpu.VMEM((CONV_TAIL, c), F32), pltpu.VMEM((n_heads, dk, dv), F32)],
        compiler_params=_params("parallel", "arbitrary"),
        name="gdn_prompt",
    )(qkv, z, ba, conv_w, alog_pad, dt_pad, norm_w)


def _gdn_step_kernel(qkv_ref, sc_ref, z_ref, ba_ref, cw_ref, alog_ref, dt_ref, nw_ref, s_ref,
                     o_ref, sn_ref, cn_ref, *, n_heads, dk, dv):
    nb, c = qkv_ref.shape
    x = qkv_ref[...]
    sc = sc_ref[...]
    cw = cw_ref[...]
    kw = cw.shape[0]
    acc = x * cw[kw - 1:kw, :]
    for i in range(kw - 1):
        acc = acc + sc[:, i * c:(i + 1) * c] * cw[i:i + 1, :]
    y = _silu(acc)
    cn_ref[...] = jnp.concatenate([sc[:, c:], x], axis=-1)

    ba = ba_ref[...]
    beta = jax.nn.sigmoid(ba)
    e_g = jnp.exp(-jnp.exp(alog_ref[...]) * _softplus(ba + dt_ref[...]))
    nw = nw_ref[...]
    pad_k = jnp.zeros((SUBLANES - 2, dk), F32)
    pad_1k = jnp.zeros((SUBLANES - 1, dk), F32)
    pad_1v = jnp.zeros((SUBLANES - 1, dv), F32)
    outs = []
    for h in range(n_heads):
        gl = n_heads + h
        qh = _l2_normalize(y[:, h * dk:(h + 1) * dk]) * (dk ** -0.5)
        kh = _l2_normalize(y[:, (n_heads + h) * dk:(n_heads + h + 1) * dk])
        vh = y[:, 2 * n_heads * dk + h * dv:2 * n_heads * dk + (h + 1) * dv]
        beta_h = beta[:, h:h + 1]
        eg_h = e_g[:, gl:gl + 1]
        qk = jnp.sum(qh * kh, axis=-1, keepdims=True)
        rows = []
        for ib in range(nb):
            s = s_ref[ib, h]
            k_row = kh[ib:ib + 1]
            lhs = jnp.concatenate([k_row, qh[ib:ib + 1], pad_k], axis=0)
            kq = _dot(lhs, s, HIGHEST)
            b_i = beta_h[ib:ib + 1]
            e_i = eg_h[ib:ib + 1]
            u = b_i * vh[ib:ib + 1] - (b_i * e_i) * kq[0:1]
            rows.append(e_i * kq[1:2] + qk[ib:ib + 1] * u)
            outer = _dot_tn(jnp.concatenate([k_row, pad_1k], axis=0), jnp.concatenate([u, pad_1v], axis=0),
                            HIGHEST)
            sn_ref[ib, h] = e_i * s + outer
        o = jnp.concatenate(rows, axis=0)
        outs.append(_rms_norm(o, nw) * _silu(z_ref[:, h * dv:(h + 1) * dv]))
    o_ref[...] = jnp.concatenate(outs, axis=-1)


def _gdn_step(qkv, conv_state, z, ba, conv_w, alog_pad, dt_pad, norm_w, state, *, n_heads, dk, dv):
    n, c = qkv.shape
    nb = _pick_tile(n, SUBLANES)
    row = lambda i: (i, 0)
    fixed = lambda i: (0, 0)
    st = lambda i: (i, 0, 0, 0)
    return pl.pallas_call(
        functools.partial(_gdn_step_kernel, n_heads=n_heads, dk=dk, dv=dv),
        out_shape=[jax.ShapeDtypeStruct((n, n_heads * dv), F32), jax.ShapeDtypeStruct(state.shape, F32),
                   jax.ShapeDtypeStruct(conv_state.shape, F32)],
        grid=(n // nb,),
        in_specs=[
            pl.BlockSpec((nb, c), row), pl.BlockSpec((nb, conv_state.shape[1]), row),
            pl.BlockSpec((nb, n_heads * dv), row), pl.BlockSpec((nb, LANES), row),
            pl.BlockSpec(conv_w.shape, fixed), pl.BlockSpec((1, LANES), fixed), pl.BlockSpec((1, LANES), fixed),
            pl.BlockSpec((1, dv), fixed), pl.BlockSpec((nb, n_heads, dk, dv), st),
        ],
        out_specs=[pl.BlockSpec((nb, n_heads * dv), row), pl.BlockSpec((nb, n_heads, dk, dv), st),
                   pl.BlockSpec((nb, conv_state.shape[1]), row)],
        compiler_params=_params("parallel"),
        name="gdn_step",
    )(qkv, conv_state, z, ba, conv_w, alog_pad, dt_pad, norm_w, state)


def _softmax_step(s, v_bf16, m, l, acc):
    m_new = jnp.maximum(m, jnp.max(s, axis=-1, keepdims=True))
    alpha = jnp.exp(m - m_new)
    p = jnp.exp(s - m_new)
    l = alpha * l + jnp.sum(p, axis=-1, keepdims=True)
    acc = alpha * acc + _dot(p.astype(BF16), v_bf16)
    return m_new, l, acc


def _attn_prompt_kernel(lamv_ref, nw_ref, q_ref, k_ref, v_ref, o_ref, *, tk, d_map, lam_init):
    iq = pl.program_id(2)
    tq, dh = q_ref.shape
    q = q_ref[...]
    lane = lax.broadcasted_iota(jnp.int32, (tq, dh), 1)
    zero = jnp.zeros_like(q)
    q2 = jnp.concatenate([jnp.where(lane < d_map, q, zero), jnp.where(lane >= d_map, q, zero)], axis=0)
    rows = 2 * tq
    n_full = (iq * tq) // tk

    def scores(j):
        return _dot_nt(q2, k_ref[pl.ds(pl.multiple_of(j * tk, tk), tk), :])

    def values(j):
        return v_ref[pl.ds(pl.multiple_of(j * tk, tk), tk), :]

    def block_group(jg, carry):
        s = [scores(ATTN_GROUP * jg + i) for i in range(ATTN_GROUP)]
        for i in range(ATTN_GROUP):
            carry = _softmax_step(s[i], values(ATTN_GROUP * jg + i), *carry)
        return carry

    def single_block(j, carry):
        return _softmax_step(scores(j), values(j), *carry)

    carry = (jnp.full((rows, 1), NEG, F32), jnp.zeros((rows, 1), F32), jnp.zeros((rows, dh), F32))
    carry = lax.fori_loop(0, n_full // ATTN_GROUP, block_group, carry)
    carry = lax.fori_loop(n_full - n_full % ATTN_GROUP, n_full, single_block, carry)

    r = lax.broadcasted_iota(jnp.int32, (rows, tk), 0)
    col_minus_row = lax.broadcasted_iota(jnp.int32, (rows, tk), 1) - jnp.where(r < tq, r, r - tq)
    s = jnp.where(col_minus_row <= iq * tq - n_full * tk, scores(n_full), NEG)
    _, l, acc = _softmax_step(s, values(n_full), *carry)
    on = acc / l
    lam = _lambda(lamv_ref[...], lam_init)
    o = on[:tq] - lam * on[tq:]
    o_ref[...] = _rms_norm(o, nw_ref[...]) * (1.0 - lam_init)


def _attn_prompt(q16, k16, v16, lamv, norm_w, *, n_heads, d_map, lam_init):
    b, t, _ = q16.shape
    dh = 2 * d_map
    tq = _pick_tile(t, 256)
    tk = _pick_tile(t, 512)
    assert tk % tq == 0, "one key block must cover a query tile's diagonal"
    fixed = lambda ib, ih, iq: (0, 0)
    return pl.pallas_call(
        functools.partial(_attn_prompt_kernel, tk=tk, d_map=d_map, lam_init=lam_init),
        out_shape=jax.ShapeDtypeStruct((b, t, n_heads * dh), F32),
        grid=(b, n_heads, t // tq),
        in_specs=[
            pl.BlockSpec(lamv.shape, fixed), pl.BlockSpec((1, dh), fixed),
            pl.BlockSpec((None, tq, dh), lambda ib, ih, iq: (ib, iq, ih)),
            pl.BlockSpec((None, t, dh), lambda ib, ih, iq: (ib, 0, ih)),
            pl.BlockSpec((None, t, dh), lambda ib, ih, iq: (ib, 0, ih)),
        ],
        out_specs=pl.BlockSpec((None, tq, dh), lambda ib, ih, iq: (ib, iq, ih)),
        compiler_params=_params("parallel", "parallel", "arbitrary"),
        name="attn_prompt",
    )(lamv, norm_w, q16, k16, v16)


def _attn_decode_kernel(pt_ref, lamv_ref, nw_ref, q_ref, kn_ref, vn_ref, *rest,
                        pages, n_heads, d_map, lam_init):
    del pt_ref
    k_refs = rest[:pages]
    v_refs = rest[pages:2 * pages]
    o_ref = rest[2 * pages]
    qmat_scr, m_scr, l_scr, acc_scr = rest[2 * pages + 1:]
    ic = pl.program_id(1)
    dh = 2 * d_map
    rows = qmat_scr.shape[0]

    def head_rows(x, mask_maps):
        lane = lax.broadcasted_iota(jnp.int32, (1, dh), 1)
        out = []
        for m in range(2):
            for h in range(n_heads):
                xh = x[:, h * dh:(h + 1) * dh]
                if mask_maps:
                    xh = jnp.where((lane >= m * d_map) & (lane < (m + 1) * d_map), xh, 0.0)
                out.append(xh)
        out.append(jnp.zeros((rows - 2 * n_heads, dh), F32))
        return jnp.concatenate(out, axis=0)

    @pl.when(ic == 0)
    def _():
        qmat_scr[...] = head_rows(q_ref[...].astype(F32), True)
        m_scr[...] = jnp.full_like(m_scr, NEG)
        l_scr[...] = jnp.zeros_like(l_scr)
        acc_scr[...] = jnp.zeros_like(acc_scr)

    k_cat = jnp.concatenate([k[...] for k in k_refs], axis=0)
    v_cat = jnp.concatenate([v[...] for v in v_refs], axis=0)

    def two_term_dot(dot, a, b):
        a_hi = a.astype(BF16)
        a_lo = (a - a_hi.astype(F32)).astype(BF16)
        b_hi = b.astype(BF16)
        b_lo = (b - b_hi.astype(F32)).astype(BF16)
        both = dot(jnp.concatenate([a_hi, a_lo], axis=0), b_hi)
        return both[:rows] + both[rows:] + dot(a_hi, b_lo)

    s = two_term_dot(_dot_nt, qmat_scr[...], k_cat)
    r = lax.broadcasted_iota(jnp.int32, s.shape, 0)
    col = lax.broadcasted_iota(jnp.int32, s.shape, 1)
    s = jnp.where(col % n_heads == r % n_heads, s, NEG)
    m_old = m_scr[...]
    m_new = jnp.maximum(m_old, jnp.max(s, axis=-1, keepdims=True))
    alpha = jnp.exp(m_old - m_new)
    p = jnp.exp(s - m_new)
    l_new = alpha * l_scr[...] + jnp.sum(p, axis=-1, keepdims=True)
    acc_new = alpha * acc_scr[...] + two_term_dot(_dot, p, v_cat)
    m_scr[...] = m_new
    l_scr[...] = l_new
    acc_scr[...] = acc_new

    @pl.when(ic == pl.num_programs(1) - 1)
    def _():
        s_new = jnp.sum(qmat_scr[...] * head_rows(kn_ref[...], False), axis=-1, keepdims=True)
        m_f = jnp.maximum(m_new, s_new)
        a = jnp.exp(m_new - m_f)
        p_new = jnp.exp(s_new - m_f)
        l_f = a * l_new + p_new
        acc_f = a * acc_new + p_new * head_rows(vn_ref[...], False)
        on = acc_f / l_f
        lam = _lambda(lamv_ref[...], lam_init)
        o = on[0:n_heads] - lam * on[n_heads:2 * n_heads]
        o = _rms_norm(o, nw_ref[...]) * (1.0 - lam_init)
        o_ref[...] = jnp.concatenate([o[h:h + 1] for h in range(n_heads)], axis=-1)


def _attn_decode(q16, k_new, v_new, cache_k, cache_v, page_table, layer, lamv, norm_w,
                 *, n_heads, d_map, lam_init):
    bs = q16.shape[0]
    dh = 2 * d_map
    depth, n_pool, page, _, _ = cache_k.shape
    n_pages = page_table.shape[1]
    pages = _pick_tile(n_pages, 8) if n_pages % 8 == 0 else 1
    page_rows = page * n_heads
    ck = cache_k.reshape(depth * n_pool, page_rows, dh)
    cv = cache_v.reshape(depth * n_pool, page_rows, dh)
    q3 = q16.reshape(bs, 1, n_heads * dh)
    kn3 = k_new.reshape(bs, 1, n_heads * dh)
    vn3 = v_new.reshape(bs, 1, n_heads * dh)
    fixed = lambda ib, ic, pt: (0, 0)
    seq = lambda ib, ic, pt: (ib, 0, 0)

    def page_map(p):
        return lambda ib, ic, pt: (layer * n_pool + pt[ib, ic * pages + p], 0, 0)

    page_specs = [pl.BlockSpec((None, page_rows, dh), page_map(p)) for p in range(pages)]
    rows = 2 * SUBLANES
    assert rows >= 2 * n_heads
    grid_spec = pltpu.PrefetchScalarGridSpec(
        num_scalar_prefetch=1,
        grid=(bs, n_pages // pages),
        in_specs=[pl.BlockSpec(lamv.shape, fixed), pl.BlockSpec((1, dh), fixed),
                  pl.BlockSpec((None, 1, n_heads * dh), seq), pl.BlockSpec((None, 1, n_heads * dh), seq),
                  pl.BlockSpec((None, 1, n_heads * dh), seq)] + page_specs + page_specs,
        out_specs=pl.BlockSpec((None, 1, n_heads * dh), seq),
        scratch_shapes=[pltpu.VMEM((rows, dh), F32), pltpu.VMEM((rows, 1), F32), pltpu.VMEM((rows, 1), F32),
                        pltpu.VMEM((rows, dh), F32)],
    )
    out = pl.pallas_call(
        functools.partial(_attn_decode_kernel, pages=pages, n_heads=n_heads, d_map=d_map, lam_init=lam_init),
        out_shape=jax.ShapeDtypeStruct((bs, 1, n_heads * dh), F32),
        grid_spec=grid_spec,
        compiler_params=_params("parallel", "arbitrary"),
        name="attn_decode",
    )(page_table, lamv, norm_w, q3, kn3, vn3, *([ck] * pages), *([cv] * pages))
    return out.reshape(bs, n_heads * dh)


def _route(logits, n_groups, n_exp):
    lane = lax.broadcasted_iota(jnp.int32, logits.shape, 1)
    ne = n_groups * n_exp
    big = jnp.int32(LANES)

    def first_max(mask):
        val = jnp.max(jnp.where(mask, logits, NEG), axis=-1, keepdims=True)
        idx = jnp.min(jnp.where(mask & (logits == val), lane, big), axis=-1, keepdims=True)
        return val, idx

    is_group = (lane >= ne) & (lane < ne + n_groups)
    g_max, g_lane = first_max(is_group)
    g_sum = jnp.sum(jnp.where(is_group, jnp.exp(jnp.where(is_group, logits, g_max) - g_max), 0.0),
                    axis=-1, keepdims=True)
    g_w = 1.0 / g_sum
    lo = (g_lane - ne) * n_exp
    in_group = (lane >= lo) & (lane < lo + n_exp)
    v1, i1 = first_max(in_group)
    v2, i2 = first_max(in_group & (lane != i1))
    e = jnp.exp(v2 - v1)
    w1 = g_w / (1.0 + e)
    w2 = g_w * e / (1.0 + e)
    return jnp.where(lane == i1, w1, jnp.where(lane == i2, w2, 0.0))


def _out_proj_kernel(oa_ref, ob_ref, x_ref, ing_ref, inb_ref, w_ref, g_ref, b_ref, wr_ref, br_ref,
                     x1_ref, x1b_ref, gate_ref, *, apply_ln, alpha, n_groups, n_exp):
    x = x_ref[...]
    if apply_ln:
        x = _layer_norm(x, ing_ref[...], inb_ref[...])
    wa = oa_ref.shape[1]
    precise = w_ref.dtype == F32
    mix_t = F32 if precise else BF16
    prec = HIGHEST if precise else None
    y = (_dot(oa_ref[...].astype(mix_t), w_ref[:wa, :], prec)
         + _dot(ob_ref[...].astype(mix_t), w_ref[wa:, :], prec))
    x1 = _layer_norm(alpha * x + y, g_ref[...], b_ref[...])
    x1_ref[...] = x1
    x_hi = x1.astype(BF16)
    x1b_ref[...] = x_hi
    x_lo = (x1 - x_hi.astype(F32)).astype(BF16)
    w_r = wr_ref[...]
    w_hi = w_r.astype(BF16)
    w_lo = (w_r - w_hi.astype(F32)).astype(BF16)
    logits = _dot(x_hi, w_hi) + _dot(x_hi, w_lo) + _dot(x_lo, w_hi) + br_ref[...]
    gate_ref[...] = _route(logits, n_groups, n_exp)


def _out_proj(o_a, o_b, x2d, in_g, in_b, w_bf16, g, b, w_router, b_router, *, apply_ln, alpha, n_groups, n_exp):
    n, d = x2d.shape
    tm = _pick_tile(n, 512)
    row = lambda i: (i, 0)
    fixed = lambda i: (0, 0)
    return pl.pallas_call(
        functools.partial(_out_proj_kernel, apply_ln=apply_ln, alpha=alpha, n_groups=n_groups, n_exp=n_exp),
        out_shape=[jax.ShapeDtypeStruct((n, d), F32), jax.ShapeDtypeStruct((n, d), BF16),
                   jax.ShapeDtypeStruct((n, LANES), F32)],
        grid=(n // tm,),
        in_specs=[
            pl.BlockSpec((tm, o_a.shape[1]), row), pl.BlockSpec((tm, o_b.shape[1]), row), pl.BlockSpec((tm, d), row),
            pl.BlockSpec((1, d), fixed), pl.BlockSpec((1, d), fixed), pl.BlockSpec(w_bf16.shape, fixed),
            pl.BlockSpec((1, d), fixed), pl.BlockSpec((1, d), fixed),
            pl.BlockSpec(w_router.shape, fixed), pl.BlockSpec((1, LANES), fixed),
        ],
        out_specs=[pl.BlockSpec((tm, d), row), pl.BlockSpec((tm, d), row), pl.BlockSpec((tm, LANES), row)],
        compiler_params=_params("parallel"),
        name="out_proj",
    )(o_a, o_b, x2d, in_g, in_b, w_bf16, g, b, w_router, b_router)


def _moe_kernel(xb_ref, x1_ref, gate_ref, wg_ref, wu_ref, wd_ref, g_ref, b_ref, o_ref, acc_ref, *, alpha):
    ie = pl.program_id(1)

    @pl.when(ie == 0)
    def _():
        acc_ref[...] = jnp.zeros_like(acc_ref)

    xb = xb_ref[...]
    gates = gate_ref[...]
    lane = lax.broadcasted_iota(jnp.int32, gates.shape, 1)
    gate_e = jnp.sum(jnp.where(lane == ie, gates, 0.0), axis=-1, keepdims=True)
    h = _silu(_dot(xb, wg_ref[...])) * _dot(xb, wu_ref[...])
    acc_ref[...] += _dot((h * gate_e).astype(BF16), wd_ref[...])

    @pl.when(ie == pl.num_programs(1) - 1)
    def _():
        o_ref[...] = _layer_norm(alpha * x1_ref[...] + acc_ref[...], g_ref[...], b_ref[...])


def _moe(x1b, x1, gates, wg, wu, wd, g, b, *, alpha):
    n, d = x1.shape
    n_exp_total, _, f = wg.shape
    tm = _pick_tile(n, 1024)
    row = lambda i, e: (i, 0)
    fixed = lambda i, e: (0, 0)
    exp = lambda i, e: (e, 0, 0)
    return pl.pallas_call(
        functools.partial(_moe_kernel, alpha=alpha),
        out_shape=jax.ShapeDtypeStruct((n, d), F32),
        grid=(n // tm, n_exp_total),
        in_specs=[
            pl.BlockSpec((tm, d), row), pl.BlockSpec((tm, d), row), pl.BlockSpec((tm, LANES), row),
            pl.BlockSpec((None, d, f), exp), pl.BlockSpec((None, d, f), exp), pl.BlockSpec((None, f, d), exp),
            pl.BlockSpec((1, d), fixed), pl.BlockSpec((1, d), fixed),
        ],
        out_specs=pl.BlockSpec((tm, d), row),
        scratch_shapes=[pltpu.VMEM((tm, d), F32)],
        compiler_params=_params("parallel", "arbitrary"),
        name="moe",
    )(x1b, x1, gates, wg, wu, wd, g, b)


def _pad_lanes(v, offset=0):
    return jnp.zeros((1, LANES), F32).at[0, offset:offset + v.shape[0]].set(v.astype(F32))


def kernel(x_prompt, x_sample, cache_k, cache_v, state_delta, state_conv, page_table, ln_in_g, ln_in_b, w_in, conv_w, a_log, dt_bias, gdn_norm_w, lambda_q1, lambda_k1, lambda_q2, lambda_k2, diff_norm_w, w_out, ln1_g, ln1_b, w_router_group, b_router_group, w_router_expert, b_router_expert, w_gate, w_up, w_down, ln2_g, ln2_b):
    bp, tp, d_model = x_prompt.shape
    bs, ts, _ = x_sample.shape
    depth = w_in.shape[0]
    n_heads_a = a_log.shape[1]
    dv = gdn_norm_w.shape[1]
    c_qkv = conv_w.shape[2]
    kw = conv_w.shape[1]
    c_z = n_heads_a * dv
    dk = (c_qkv - c_z) // (2 * n_heads_a)
    n_heads_b = cache_k.shape[3]
    d_map = lambda_q1.shape[1]
    dh = 2 * d_map
    c_b = n_heads_b * dh
    rope_dim = d_map // 4
    n_groups, n_exp = w_gate.shape[1], w_gate.shape[2]
    d_expert = w_gate.shape[4]
    n_pages = page_table.shape[1]
    past_len = n_pages * cache_k.shape[2]
    alpha = (2.0 * depth) ** 0.25
    q_scale = d_map ** -0.5

    assert ts == 1, "one new token per sample sequence"
    assert dh == LANES and dk == LANES and dv == LANES, "heads are one vreg wide"
    assert math.frexp(q_scale)[0] == 0.5, "score scale must be a power of two to fold into q exactly"
    assert 2 * n_heads_a <= LANES and n_groups * n_exp + n_groups <= LANES
    assert w_in.shape[2] == c_qkv + c_z + 2 * n_heads_a + 3 * c_b

    tabs_p = _rope_tables(jnp.arange(tp), d_map, rope_dim)
    tabs_s = tuple(jnp.tile(t, (bs, 1)) for t in _rope_tables(past_len + jnp.arange(ts), d_map, rope_dim))
    in_g, in_b = ln_in_g.reshape(1, d_model), ln_in_b.reshape(1, d_model)

    xp = x_prompt.reshape(bp * tp, d_model)
    xs = x_sample.reshape(bs * ts, d_model)
    outs = [[] for _ in range(8)]
    for l in range(depth):
        lam_init = 0.8 - 0.6 * math.exp(-0.3 * l)
        first = l == 0
        w = w_in[l]
        o_ba = c_qkv + c_z
        o_b = o_ba + 2 * n_heads_a
        w_ba = jnp.pad(w[:, o_ba:o_b], ((0, 0), (0, LANES - 2 * n_heads_a)))
        w_in_f32 = jnp.concatenate([w[:, :o_ba], w[:, o_b:], w_ba], axis=1)
        w_in_l = w_in_f32.astype(BF16)
        w_out_l = w_out[l].astype(BF16)
        ne = n_groups * n_exp
        w_r = jnp.pad(jnp.concatenate([w_router_expert[l], w_router_group[l]], axis=1),
                      ((0, 0), (0, LANES - ne - n_groups)))
        b_r = _pad_lanes(jnp.concatenate([b_router_expert[l], b_router_group[l]]))
        wg = w_gate[l].reshape(ne, d_model, d_expert).astype(BF16)
        wu = w_up[l].reshape(ne, d_model, d_expert).astype(BF16)
        wd = w_down[l].reshape(ne, d_expert, d_model).astype(BF16)
        alog_pad = _pad_lanes(a_log[l], n_heads_a)
        dt_pad = _pad_lanes(dt_bias[l], n_heads_a)
        gdn_w = gdn_norm_w[l].reshape(1, dv)
        diff_w = diff_norm_w[l].reshape(1, dh)
        lamv = jnp.concatenate([_pad_lanes(v) for v in (lambda_q1[l], lambda_k1[l], lambda_q2[l], lambda_k2[l])]
                               + [jnp.zeros((SUBLANES - 4, LANES), F32)], axis=0)
        ln1 = (ln1_g[l].reshape(1, d_model), ln1_b[l].reshape(1, d_model))
        ln2 = (ln2_g[l].reshape(1, d_model), ln2_b[l].reshape(1, d_model))
        proj = functools.partial(_in_proj, apply_ln=first, c_qkv=c_qkv, c_z=c_z, c_b=c_b, q_scale=q_scale,
                                 rope_half=rope_dim // 2)
        mix_out = functools.partial(_out_proj, apply_ln=first, alpha=alpha, n_groups=n_groups, n_exp=n_exp)
        heads_a = dict(n_heads=n_heads_a, dk=dk, dv=dv)
        heads_b = dict(n_heads=n_heads_b, d_map=d_map, lam_init=lam_init)

        qkv, z, ba, q16, k32, k16, v32, v16 = proj(xp, in_g, in_b, w_in_l, tabs_p)
        shape3 = lambda a: a.reshape(bp, tp, a.shape[1])
        o_a, s_new = _gdn_prompt(shape3(qkv), shape3(z), shape3(ba), conv_w[l], alog_pad, dt_pad, gdn_w, **heads_a)
        o_bp = _attn_prompt(shape3(q16), shape3(k16), shape3(v16), lamv, diff_w, **heads_b)
        x1, x1b, gates = mix_out(o_a.reshape(bp * tp, c_z), o_bp.reshape(bp * tp, c_b), xp, in_g, in_b,
                                 w_out_l, *ln1, w_r, b_r)
        conv_rows = shape3(qkv)[:, tp - (kw - 1):, :]
        if tp < kw - 1:
            conv_rows = jnp.pad(shape3(qkv), ((0, 0), (kw - 1 - tp, 0), (0, 0)))
        xp = _moe(x1b, x1, gates, wg, wu, wd, *ln2, alpha=alpha)
        for i, a in enumerate((k32.reshape(bp, tp, n_heads_b, dh), v32.reshape(bp, tp, n_heads_b, dh),
                               s_new, conv_rows)):
            outs[i].append(a)

        qkv, z, ba, q16, k32, _, v32, _ = proj(xs, in_g, in_b, w_in_f32, tabs_s)
        o_a, s_new, c_new = _gdn_step(qkv, state_conv[l].reshape(bs, (kw - 1) * c_qkv), z, ba, conv_w[l],
                                      alog_pad, dt_pad, gdn_w, state_delta[l], **heads_a)
        o_bs = _attn_decode(q16, k32, v32, cache_k, cache_v, page_table, l, lamv, diff_w, **heads_b)
        x1, x1b, gates = mix_out(o_a, o_bs, xs, in_g, in_b, w_out[l], *ln1, w_r, b_r)
        xs = _moe(x1b, x1, gates, wg, wu, wd, *ln2, alpha=alpha)
        for i, a in enumerate((k32.reshape(bs, ts, n_heads_b, dh), v32.reshape(bs, ts, n_heads_b, dh),
                               s_new, c_new.reshape(bs, kw - 1, c_qkv))):
            outs[4 + i].append(a)

    return (xp.reshape(bp, tp, d_model), xs.reshape(bs, ts, d_model)) + tuple(jnp.stack(o) for o in outs)
```

```python
import functools
import math

import jax
import jax.numpy as jnp
from jax import lax
from jax.experimental import pallas as pl
from jax.experimental.pallas import tpu as pltpu

F32 = jnp.float32
BF16 = jnp.bfloat16
HIGHEST = lax.Precision.HIGHEST

LANES = 128
SUBLANES = 8
VMEM_LIMIT_BYTES = 56 * 1024 * 1024

LN_EPS = 1e-5
RMS_EPS = 1e-6
ROPE_THETA = 500000.0
CONV_TAIL = SUBLANES
GDN_CHUNK = 64
GDN_GROUP = 4
DECODE_PAGES = 8
ATTN_GROUP = 4
NEG = -1e30


def _pick_tile(n, preferred):
    if n <= preferred:
        return n
    for t in range(preferred, 0, -1):
        if n % t == 0 and t % SUBLANES == 0:
            return t
    return n


def _params(*semantics):
    return pltpu.CompilerParams(dimension_semantics=semantics, vmem_limit_bytes=VMEM_LIMIT_BYTES)


def _dot(a, b, precision=None):
    return jnp.dot(a, b, preferred_element_type=F32, precision=precision)


def _dot_nt(a, b, precision=None):
    return lax.dot_general(a, b, (((1,), (1,)), ((), ())), preferred_element_type=F32, precision=precision)


def _dot_tn(a, b, precision=None):
    return lax.dot_general(a, b, (((0,), (0,)), ((), ())), preferred_element_type=F32, precision=precision)


def _layer_norm(x, g, b):
    mu = jnp.mean(x, axis=-1, keepdims=True)
    xc = x - mu
    var = jnp.mean(xc * xc, axis=-1, keepdims=True)
    return xc * lax.rsqrt(var + LN_EPS) * g + b


def _rms_norm(x, w):
    return x * lax.rsqrt(jnp.mean(x * x, axis=-1, keepdims=True) + RMS_EPS) * w


def _l2_normalize(x):
    return x * lax.rsqrt(jnp.sum(x * x, axis=-1, keepdims=True) + RMS_EPS)


def _silu(x):
    return x * jax.nn.sigmoid(x)


def _softplus(x):
    return jnp.maximum(x, 0.0) + jnp.log1p(jnp.exp(-jnp.abs(x)))


def _lambda(lamv, lam_init):
    s1 = jnp.sum(lamv[0:1] * lamv[1:2], axis=-1, keepdims=True)
    s2 = jnp.sum(lamv[2:3] * lamv[3:4], axis=-1, keepdims=True)
    return jnp.exp(s1) - jnp.exp(s2) + lam_init


def _in_proj_kernel(x_ref, g_ref, b_ref, w_ref, cos_ref, sa_ref, sb_ref,
                    qkv_ref, z_ref, ba_ref, q16_ref, k32_ref, k16_ref, v32_ref, v16_ref,
                    *, apply_ln, c_qkv, c_z, c_b, q_scale, rope_half):
    x = x_ref[...]
    if apply_ln:
        x = _layer_norm(x, g_ref[...], b_ref[...])
    precise = w_ref.dtype == F32
    xin = x if precise else x.astype(BF16)

    def proj(lo, width):
        return _dot(xin, w_ref[:, lo:lo + width], HIGHEST if precise else None)

    qkv_ref[...] = proj(0, c_qkv)
    z_ref[...] = proj(c_qkv, c_z)
    base = c_qkv + c_z
    q = proj(base, c_b)
    k = proj(base + c_b, c_b)
    v = proj(base + 2 * c_b, c_b)
    ba_ref[...] = proj(base + 3 * c_b, LANES)

    cos = cos_ref[...]
    sa = sa_ref[...]
    sb = sb_ref[...]

    def rope(y):
        outs = []
        for h in range(c_b // LANES):
            yh = y[:, h * LANES:(h + 1) * LANES]
            up = pltpu.roll(yh, LANES - rope_half, 1)
            dn = pltpu.roll(yh, rope_half, 1)
            outs.append(yh * cos + up * sa + dn * sb)
        return jnp.concatenate(outs, axis=-1)

    q = rope(q) * q_scale
    k = rope(k)
    q16_ref[...] = q.astype(q16_ref.dtype)
    k16_ref[...] = k.astype(BF16)
    v16_ref[...] = v.astype(BF16)
    tm = x_ref.shape[0]
    n_b = c_b // LANES
    for h in range(n_b):
        k32_ref[pl.ds(h, tm, stride=n_b), :] = k[:, h * LANES:(h + 1) * LANES]
        v32_ref[pl.ds(h, tm, stride=n_b), :] = v[:, h * LANES:(h + 1) * LANES]


def _in_proj(x2d, ln_g, ln_b, w_bf16, tabs, *, apply_ln, c_qkv, c_z, c_b, q_scale, rope_half):
    n, d = x2d.shape
    cos, sa, sb = tabs
    tm = _pick_tile(math.gcd(n, cos.shape[0]), 512)
    tab_tiles = cos.shape[0] // tm
    row = lambda i: (i, 0)
    fixed = lambda i: (0, 0)
    tab = lambda i: (i % tab_tiles, 0)
    n_b = c_b // LANES
    outs = [
        jax.ShapeDtypeStruct((n, c_qkv), F32), jax.ShapeDtypeStruct((n, c_z), F32),
        jax.ShapeDtypeStruct((n, LANES), F32), jax.ShapeDtypeStruct((n, c_b), w_bf16.dtype),
        jax.ShapeDtypeStruct((n * n_b, LANES), F32), jax.ShapeDtypeStruct((n, c_b), BF16),
        jax.ShapeDtypeStruct((n * n_b, LANES), F32), jax.ShapeDtypeStruct((n, c_b), BF16),
    ]
    return pl.pallas_call(
        functools.partial(_in_proj_kernel, apply_ln=apply_ln, c_qkv=c_qkv, c_z=c_z, c_b=c_b,
                          q_scale=q_scale, rope_half=rope_half),
        out_shape=outs,
        grid=(n // tm,),
        in_specs=[
            pl.BlockSpec((tm, d), row), pl.BlockSpec((1, d), fixed), pl.BlockSpec((1, d), fixed),
            pl.BlockSpec(w_bf16.shape, fixed),
            pl.BlockSpec((tm, LANES), tab), pl.BlockSpec((tm, LANES), tab), pl.BlockSpec((tm, LANES), tab),
        ],
        out_specs=[pl.BlockSpec((tm * o.shape[0] // n, o.shape[1]), row) for o in outs],
        compiler_params=_params("parallel"),
        name="in_proj",
    )(x2d, ln_g, ln_b, w_bf16, cos, sa, sb)


def _rope_tables(pos, d_map, rope_dim):
    half = rope_dim // 2
    inv = ROPE_THETA ** (-jnp.arange(0, rope_dim, 2, dtype=F32) / rope_dim)
    ang = pos.astype(F32)[:, None] * inv
    cos, sin = jnp.cos(ang), jnp.sin(ang)
    n = pos.shape[0]
    ones = jnp.ones((n, d_map - rope_dim), F32)
    zeros_rest = jnp.zeros((n, d_map - rope_dim), F32)
    zeros_half = jnp.zeros((n, half), F32)
    cos_m = jnp.concatenate([cos, cos, ones], axis=1)
    sa_m = jnp.concatenate([-sin, zeros_half, zeros_rest], axis=1)
    sb_m = jnp.concatenate([zeros_half, sin, zeros_rest], axis=1)
    return tuple(jnp.concatenate([t, t], axis=1) for t in (cos_m, sa_m, sb_m))


def _split3(x):
    x1 = x.astype(BF16)
    r1 = x - x1.astype(F32)
    x2 = r1.astype(BF16)
    x3 = (r1 - x2.astype(F32)).astype(BF16)
    return x1, x2, x3


def _neumann_series(ms):
    n = ms[0].shape[0]
    accs = list(ms)
    qs = [_dot(m.astype(BF16), m.astype(BF16)) for m in ms]
    span = 4
    while span < n:
        rs = [_dot(jnp.concatenate([a.astype(BF16), q.astype(BF16)], axis=0), q.astype(BF16))
              for a, q in zip(accs, qs)]
        accs = [a + q + r[:n] for a, q, r in zip(accs, qs, rs)]
        qs = [r[n:] for r in rs]
        span *= 2
    return [a + q + _dot(a.astype(BF16), q.astype(BF16)) for a, q in zip(accs, qs)]


def _gdn_prompt_kernel(qkv_ref, z_ref, ba_ref, cw_ref, alog_ref, dt_ref, nw_ref,
                       o_ref, s_ref, y_scr, gam_scr, beta_scr, prev_scr, state_scr,
                       *, n_heads, dk, dv, chunk, group):
    t = pl.program_id(1)
    tt, c = qkv_ref.shape

    @pl.when(t == 0)
    def _():
        prev_scr[...] = jnp.zeros_like(prev_scr)
        state_scr[...] = jnp.zeros_like(state_scr)

    x = qkv_ref[...]
    prev = prev_scr[...]
    cw = cw_ref[...]
    kw = cw.shape[0]
    tail_row = lax.broadcasted_iota(jnp.int32, (CONV_TAIL, c), 0)
    acc = x * cw[kw - 1:kw, :]
    for j in range(1, kw):
        xs = pltpu.roll(x, j, 0)
        ps = pltpu.roll(prev, j, 0)
        head = jnp.where(tail_row < j, ps, xs[0:CONV_TAIL])
        xs = jnp.concatenate([head, xs[CONV_TAIL:]], axis=0)
        acc = acc + xs * cw[kw - 1 - j:kw - j, :]
    prev_scr[...] = x[tt - CONV_TAIL:tt]
    y_scr[...] = _silu(acc)

    ba = ba_ref[...]
    beta_scr[...] = jax.nn.sigmoid(ba)
    gam = -jnp.exp(alog_ref[...]) * _softplus(ba + dt_ref[...])
    row_in_chunk = lax.broadcasted_iota(jnp.int32, (tt, LANES), 0) % chunk
    step = 1
    while step < chunk:
        gam = gam + jnp.where(row_in_chunk >= step, pltpu.roll(gam, step, 0), 0.0)
        step *= 2
    gam_scr[...] = gam

    ri = lax.broadcasted_iota(jnp.int32, (chunk, chunk), 0)
    ci = lax.broadcasted_iota(jnp.int32, (chunk, chunk), 1)
    incl = ri >= ci
    strict = ri > ci
    sel_r = lax.broadcasted_iota(jnp.int32, (SUBLANES, LANES), 0)
    sel_c = lax.broadcasted_iota(jnp.int32, (SUBLANES, LANES), 1)
    head_rows = (sel_c == sel_r + n_heads).astype(BF16)
    nw = nw_ref[...]

    def chunk_group(ig, carry):
        starts = [pl.multiple_of((ig * group + c) * chunk, chunk) for c in range(group)]
        units = []
        for r0 in starts:
            gam = gam_scr[pl.ds(r0, chunk), :]
            b_c = beta_scr[pl.ds(r0, chunk), :]
            y_c = y_scr[pl.ds(r0, chunk), :]
            picked = _dot_nt(head_rows, jnp.concatenate(_split3(gam), axis=0))
            gam_rows = picked[:, :chunk] + picked[:, chunk:2 * chunk] + picked[:, 2 * chunk:]
            e_gam = jnp.exp(gam)
            g_last = gam[chunk - 1:chunk, :]
            k_end_scale = jnp.exp(g_last - gam)
            g_end = jnp.exp(g_last)
            for h in range(n_heads):
                gl = n_heads + h
                qh = _l2_normalize(y_c[:, h * dk:(h + 1) * dk]) * (dk ** -0.5)
                kh = _l2_normalize(y_c[:, (n_heads + h) * dk:(n_heads + h + 1) * dk])
                vh = y_c[:, 2 * n_heads * dk + h * dv:2 * n_heads * dk + (h + 1) * dv]
                diff = gam[:, gl:gl + 1] - gam_rows[h:h + 1, :]
                units.append(dict(
                    qh=qh, kh=kh, vh=vh, beta=b_c[:, h:h + 1], eg=e_gam[:, gl:gl + 1],
                    decay=jnp.where(incl, jnp.exp(jnp.where(incl, diff, 0.0)), 0.0),
                    k_end=k_end_scale[:, gl:gl + 1] * kh, g_end=g_end[:, gl:gl + 1]))
        qk_kk = [_dot_nt(jnp.concatenate([u["qh"], u["kh"]], axis=0).astype(BF16), u["kh"].astype(BF16))
                 for u in units]
        attn = [x[:chunk] * u["decay"] for x, u in zip(qk_kk, units)]
        ms = [jnp.where(strict, -(u["beta"] * x[chunk:] * u["decay"]), 0.0) for x, u in zip(qk_kk, units)]
        inv_minus_i = _neumann_series(ms)
        rhs = [jnp.concatenate([u["beta"] * u["vh"], (u["beta"] * u["eg"]) * u["kh"]], axis=-1) for u in units]
        sols = [r + _dot(n.astype(BF16), r.astype(BF16)) for n, r in zip(inv_minus_i, rhs)]

        states = [state_scr[h] for h in range(n_heads)]
        for c, r0 in enumerate(starts):
            ids = range(c * n_heads, (c + 1) * n_heads)
            wq = [_dot(jnp.concatenate([sols[i][:, dv:], units[i]["eg"] * units[i]["qh"]], axis=0).astype(BF16),
                       s.astype(BF16)) for i, s in zip(ids, states)]
            ub = [(sols[i][:, :dv] - x[:chunk]).astype(BF16) for i, x in zip(ids, wq)]
            o = [x[chunk:] + _dot(attn[i].astype(BF16), u) for i, x, u in zip(ids, wq, ub)]
            states = [units[i]["g_end"] * s + _dot_tn(units[i]["k_end"].astype(BF16), u)
                      for i, s, u in zip(ids, states, ub)]
            z_c = z_ref[pl.ds(r0, chunk), :]
            o_ref[pl.ds(r0, chunk), :] = jnp.concatenate(
                [_rms_norm(x, nw) * _silu(z_c[:, h * dv:(h + 1) * dv]) for h, x in enumerate(o)], axis=-1)
        for h in range(n_heads):
            state_scr[h] = states[h]
        return carry

    lax.fori_loop(0, tt // (group * chunk), chunk_group, 0)

    @pl.when(t == pl.num_programs(1) - 1)
    def _():
        s_ref[...] = state_scr[...]


def _gdn_prompt(qkv, z, ba, conv_w, alog_pad, dt_pad, norm_w, *, n_heads, dk, dv):
    b, t, c = qkv.shape
    tt = _pick_tile(t, 512)
    group = math.gcd(tt // GDN_CHUNK, GDN_GROUP)
    assert tt % (group * GDN_CHUNK) == 0 and tt >= CONV_TAIL
    tile = lambda ib, it: (ib, it, 0)
    fixed = lambda ib, it: (0, 0)
    return pl.pallas_call(
        functools.partial(_gdn_prompt_kernel, n_heads=n_heads, dk=dk, dv=dv, chunk=GDN_CHUNK, group=group),
        out_shape=[jax.ShapeDtypeStruct((b, t, n_heads * dv), F32),
                   jax.ShapeDtypeStruct((b, n_heads, dk, dv), F32)],
        grid=(b, t // tt),
        in_specs=[
            pl.BlockSpec((None, tt, c), tile), pl.BlockSpec((None, tt, n_heads * dv), tile),
            pl.BlockSpec((None, tt, LANES), tile),
            pl.BlockSpec(conv_w.shape, fixed), pl.BlockSpec((1, LANES), fixed), pl.BlockSpec((1, LANES), fixed),
            pl.BlockSpec((1, dv), fixed),
        ],
        out_specs=[pl.BlockSpec((None, tt, n_heads * dv), tile),
                   pl.BlockSpec((None, n_heads, dk, dv), lambda ib, it: (ib, 0, 0, 0))],
        scratch_shapes=[pltpu.VMEM((tt, c), F32), pltpu.VMEM((tt, LANES), F32), pltpu.VMEM((tt, LANES), F32),
                        pltpu.VMEM((CONV_TAIL, c), F32), pltpu.VMEM((n_heads, dk, dv), F32)],
        compiler_params=_params("parallel", "arbitrary"),
        name="gdn_prompt",
    )(qkv, z, ba, conv_w, alog_pad, dt_pad, norm_w)


def _gdn_step_kernel(qkv_ref, sc_ref, z_ref, ba_ref, cw_ref, alog_ref, dt_ref, nw_ref, s_ref,
                     o_ref, sn_ref, cn_ref, *, n_heads, dk, dv):
    nb, c = qkv_ref.shape
    x = qkv_ref[...]
    sc = sc_ref[...]
    cw = cw_ref[...]
    kw = cw.shape[0]
    acc = x * cw[kw - 1:kw, :]
    for i in range(kw - 1):
        acc = acc + sc[:, i * c:(i + 1) * c] * cw[i:i + 1, :]
    y = _silu(acc)
    cn_ref[...] = jnp.concatenate([sc[:, c:], x], axis=-1)

    ba = ba_ref[...]
    beta = jax.nn.sigmoid(ba)
    e_g = jnp.exp(-jnp.exp(alog_ref[...]) * _softplus(ba + dt_ref[...]))
    nw = nw_ref[...]
    pad_k = jnp.zeros((SUBLANES - 2, dk), F32)
    pad_1k = jnp.zeros((SUBLANES - 1, dk), F32)
    pad_1v = jnp.zeros((SUBLANES - 1, dv), F32)
    outs = []
    for h in range(n_heads):
        gl = n_heads + h
        qh = _l2_normalize(y[:, h * dk:(h + 1) * dk]) * (dk ** -0.5)
        kh = _l2_normalize(y[:, (n_heads + h) * dk:(n_heads + h + 1) * dk])
        vh = y[:, 2 * n_heads * dk + h * dv:2 * n_heads * dk + (h + 1) * dv]
        beta_h = beta[:, h:h + 1]
        eg_h = e_g[:, gl:gl + 1]
        qk = jnp.sum(qh * kh, axis=-1, keepdims=True)
        rows = []
        for ib in range(nb):
            s = s_ref[ib, h]
            k_row = kh[ib:ib + 1]
            lhs = jnp.concatenate([k_row, qh[ib:ib + 1], pad_k], axis=0)
            kq = _dot(lhs, s, HIGHEST)
            b_i = beta_h[ib:ib + 1]
            e_i = eg_h[ib:ib + 1]
            u = b_i * vh[ib:ib + 1] - (b_i * e_i) * kq[0:1]
            rows.append(e_i * kq[1:2] + qk[ib:ib + 1] * u)
            outer = _dot_tn(jnp.concatenate([k_row, pad_1k], axis=0), jnp.concatenate([u, pad_1v], axis=0),
                            HIGHEST)
            sn_ref[ib, h] = e_i * s + outer
        o = jnp.concatenate(rows, axis=0)
        outs.append(_rms_norm(o, nw) * _silu(z_ref[:, h * dv:(h + 1) * dv]))
    o_ref[...] = jnp.concatenate(outs, axis=-1)


def _gdn_step(qkv, conv_state, z, ba, conv_w, alog_pad, dt_pad, norm_w, state, *, n_heads, dk, dv):
    n, c = qkv.shape
    nb = _pick_tile(n, SUBLANES)
    row = lambda i: (i, 0)
    fixed = lambda i: (0, 0)
    st = lambda i: (i, 0, 0, 0)
    return pl.pallas_call(
        functools.partial(_gdn_step_kernel, n_heads=n_heads, dk=dk, dv=dv),
        out_shape=[jax.ShapeDtypeStruct((n, n_heads * dv), F32), jax.ShapeDtypeStruct(state.shape, F32),
                   jax.ShapeDtypeStruct(conv_state.shape, F32)],
        grid=(n // nb,),
        in_specs=[
            pl.BlockSpec((nb, c), row), pl.BlockSpec((nb, conv_state.shape[1]), row),
            pl.BlockSpec((nb, n_heads * dv), row), pl.BlockSpec((nb, LANES), row),
            pl.BlockSpec(conv_w.shape, fixed), pl.BlockSpec((1, LANES), fixed), pl.BlockSpec((1, LANES), fixed),
            pl.BlockSpec((1, dv), fixed), pl.BlockSpec((nb, n_heads, dk, dv), st),
        ],
        out_specs=[pl.BlockSpec((nb, n_heads * dv), row), pl.BlockSpec((nb, n_heads, dk, dv), st),
                   pl.BlockSpec((nb, conv_state.shape[1]), row)],
        compiler_params=_params("parallel"),
        name="gdn_step",
    )(qkv, conv_state, z, ba, conv_w, alog_pad, dt_pad, norm_w, state)


def _softmax_step(s, v_bf16, m, l, acc):
    m_new = jnp.maximum(m, jnp.max(s, axis=-1, keepdims=True))
    alpha = jnp.exp(m - m_new)
    p = jnp.exp(s - m_new)
    l = alpha * l + jnp.sum(p, axis=-1, keepdims=True)
    acc = alpha * acc + _dot(p.astype(BF16), v_bf16)
    return m_new, l, acc


def _attn_prompt_kernel(lamv_ref, nw_ref, q_ref, k_ref, v_ref, o_ref, *, tk, d_map, lam_init):
    iq = pl.program_id(2)
    tq, dh = q_ref.shape
    q = q_ref[...]
    lane = lax.broadcasted_iota(jnp.int32, (tq, dh), 1)
    zero = jnp.zeros_like(q)
    q2 = jnp.concatenate([jnp.where(lane < d_map, q, zero), jnp.where(lane >= d_map, q, zero)], axis=0)
    rows = 2 * tq
    n_full = (iq * tq) // tk

    def scores(j):
        return _dot_nt(q2, k_ref[pl.ds(pl.multiple_of(j * tk, tk), tk), :])

    def values(j):
        return v_ref[pl.ds(pl.multiple_of(j * tk, tk), tk), :]

    def block_group(jg, carry):
        s = [scores(ATTN_GROUP * jg + i) for i in range(ATTN_GROUP)]
        for i in range(ATTN_GROUP):
            carry = _softmax_step(s[i], values(ATTN_GROUP * jg + i), *carry)
        return carry

    def single_block(j, carry):
        return _softmax_step(scores(j), values(j), *carry)

    carry = (jnp.full((rows, 1), NEG, F32), jnp.zeros((rows, 1), F32), jnp.zeros((rows, dh), F32))
    carry = lax.fori_loop(0, n_full // ATTN_GROUP, block_group, carry)
    carry = lax.fori_loop(n_full - n_full % ATTN_GROUP, n_full, single_block, carry)

    r = lax.broadcasted_iota(jnp.int32, (rows, tk), 0)
    col_minus_row = lax.broadcasted_iota(jnp.int32, (rows, tk), 1) - jnp.where(r < tq, r, r - tq)
    s = jnp.where(col_minus_row <= iq * tq - n_full * tk, scores(n_full), NEG)
    _, l, acc = _softmax_step(s, values(n_full), *carry)
    on = acc / l
    lam = _lambda(lamv_ref[...], lam_init)
    o = on[:tq] - lam * on[tq:]
    o_ref[...] = _rms_norm(o, nw_ref[...]) * (1.0 - lam_init)


def _attn_prompt(q16, k16, v16, lamv, norm_w, *, n_heads, d_map, lam_init):
    b, t, _ = q16.shape
    dh = 2 * d_map
    tq = _pick_tile(t, 256)
    tk = _pick_tile(t, 512)
    assert tk % tq == 0, "one key block must cover a query tile's diagonal"
    fixed = lambda ib, ih, iq: (0, 0)
    return pl.pallas_call(
        functools.partial(_attn_prompt_kernel, tk=tk, d_map=d_map, lam_init=lam_init),
        out_shape=jax.ShapeDtypeStruct((b, t, n_heads * dh), F32),
        grid=(b, n_heads, t // tq),
        in_specs=[
            pl.BlockSpec(lamv.shape, fixed), pl.BlockSpec((1, dh), fixed),
            pl.BlockSpec((None, tq, dh), lambda ib, ih, iq: (ib, iq, ih)),
            pl.BlockSpec((None, t, dh), lambda ib, ih, iq: (ib, 0, ih)),
            pl.BlockSpec((None, t, dh), lambda ib, ih, iq: (ib, 0, ih)),
        ],
        out_specs=pl.BlockSpec((None, tq, dh), lambda ib, ih, iq: (ib, iq, ih)),
        compiler_params=_params("parallel", "parallel", "arbitrary"),
        name="attn_prompt",
    )(lamv, norm_w, q16, k16, v16)


def _attn_decode_kernel(pt_ref, lamv_ref, nw_ref, q_ref, kn_ref, vn_ref, ck_ref, cv_ref, o_ref,
                        kbuf, vbuf, sems, *, pages, page_base, n_heads, d_map, lam_init):
    ib = pl.program_id(0)
    n_seq = pl.num_programs(0)
    n_chunks = pt_ref.shape[1] // pages
    dh = 2 * d_map
    page_rows = kbuf.shape[1] // pages
    rows = 2 * SUBLANES

    def page_copies(seq, chunk, slot):
        copies = []
        for p in range(pages):
            page = page_base + pt_ref[seq, chunk * pages + p]
            dst = pl.ds(p * page_rows, page_rows)
            copies.append(pltpu.make_async_copy(ck_ref.at[page], kbuf.at[slot, dst, :], sems.at[slot, 0]))
            copies.append(pltpu.make_async_copy(cv_ref.at[page], vbuf.at[slot, dst, :], sems.at[slot, 1]))
        return copies

    def head_rows(x, mask_maps):
        lane = lax.broadcasted_iota(jnp.int32, (1, dh), 1)
        out = []
        for m in range(2):
            for h in range(n_heads):
                xh = x[:, h * dh:(h + 1) * dh]
                if mask_maps:
                    xh = jnp.where((lane >= m * d_map) & (lane < (m + 1) * d_map), xh, 0.0)
                out.append(xh)
        out.append(jnp.zeros((rows - 2 * n_heads, dh), F32))
        return jnp.concatenate(out, axis=0)

    def two_term_dot(dot, a, b):
        a_hi = a.astype(BF16)
        a_lo = (a - a_hi.astype(F32)).astype(BF16)
        b_hi = b.astype(BF16)
        b_lo = (b - b_hi.astype(F32)).astype(BF16)
        both = dot(jnp.concatenate([a_hi, a_lo], axis=0), b_hi)
        return both[:rows] + both[rows:] + dot(a_hi, b_lo)

    @pl.when(ib == 0)
    def _():
        for cp in page_copies(0, 0, 0):
            cp.start()

    qmat = head_rows(q_ref[...].astype(F32), True)
    n_cols = pages * page_rows
    own_head = (lax.broadcasted_iota(jnp.int32, (rows, n_cols), 1) % n_heads
                == lax.broadcasted_iota(jnp.int32, (rows, n_cols), 0) % n_heads)

    def chunk_body(ic, carry):
        m_old, l_old, acc_old = carry
        slot = (ib * n_chunks + ic) % 2
        last = ic == n_chunks - 1
        next_seq = jnp.where(last, ib + 1, ib)
        next_chunk = jnp.where(last, 0, ic + 1)

        @pl.when(next_seq < n_seq)
        def _():
            for cp in page_copies(next_seq, next_chunk, 1 - slot):
                cp.start()

        for cp in page_copies(ib, ic, slot):
            cp.wait()
        s = jnp.where(own_head, two_term_dot(_dot_nt, qmat, kbuf[slot]), NEG)
        m_new = jnp.maximum(m_old, jnp.max(s, axis=-1, keepdims=True))
        alpha = jnp.exp(m_old - m_new)
        p = jnp.exp(s - m_new)
        l_new = alpha * l_old + jnp.sum(p, axis=-1, keepdims=True)
        acc_new = alpha * acc_old + two_term_dot(_dot, p, vbuf[slot])
        return m_new, l_new, acc_new

    init = (jnp.full((rows, 1), NEG, F32), jnp.zeros((rows, 1), F32), jnp.zeros((rows, dh), F32))
    m_c, l_c, acc_c = lax.fori_loop(0, n_chunks, chunk_body, init)

    s_new = jnp.sum(qmat * head_rows(kn_ref[...], False), axis=-1, keepdims=True)
    m_f = jnp.maximum(m_c, s_new)
    a = jnp.exp(m_c - m_f)
    p_new = jnp.exp(s_new - m_f)
    l_f = a * l_c + p_new
    acc_f = a * acc_c + p_new * head_rows(vn_ref[...], False)
    on = acc_f / l_f
    lam = _lambda(lamv_ref[...], lam_init)
    o = on[0:n_heads] - lam * on[n_heads:2 * n_heads]
    o = _rms_norm(o, nw_ref[...]) * (1.0 - lam_init)
    o_ref[...] = jnp.concatenate([o[h:h + 1] for h in range(n_heads)], axis=-1)


def _attn_decode(q, k_new, v_new, cache_k, cache_v, page_table, layer, lamv, norm_w,
                 *, n_heads, d_map, lam_init):
    bs = q.shape[0]
    dh = 2 * d_map
    depth, n_pool, page, _, _ = cache_k.shape
    n_pages = page_table.shape[1]
    pages = math.gcd(n_pages, DECODE_PAGES)
    page_rows = page * n_heads
    assert 2 * SUBLANES >= 2 * n_heads
    ck = cache_k.reshape(depth * n_pool, page_rows, dh)
    cv = cache_v.reshape(depth * n_pool, page_rows, dh)
    q3 = q.reshape(bs, 1, n_heads * dh)
    kn3 = k_new.reshape(bs, 1, n_heads * dh)
    vn3 = v_new.reshape(bs, 1, n_heads * dh)
    fixed = lambda ib, pt: (0, 0)
    seq = lambda ib, pt: (ib, 0, 0)
    grid_spec = pltpu.PrefetchScalarGridSpec(
        num_scalar_prefetch=1,
        grid=(bs,),
        in_specs=[pl.BlockSpec(lamv.shape, fixed), pl.BlockSpec((1, dh), fixed),
                  pl.BlockSpec((None, 1, n_heads * dh), seq), pl.BlockSpec((None, 1, n_heads * dh), seq),
                  pl.BlockSpec((None, 1, n_heads * dh), seq),
                  pl.BlockSpec(memory_space=pl.ANY), pl.BlockSpec(memory_space=pl.ANY)],
        out_specs=pl.BlockSpec((None, 1, n_heads * dh), seq),
        scratch_shapes=[pltpu.VMEM((2, pages * page_rows, dh), F32), pltpu.VMEM((2, pages * page_rows, dh), F32),
                        pltpu.SemaphoreType.DMA((2, 2))],
    )
    out = pl.pallas_call(
        functools.partial(_attn_decode_kernel, pages=pages, page_base=layer * n_pool, n_heads=n_heads,
                          d_map=d_map, lam_init=lam_init),
        out_shape=jax.ShapeDtypeStruct((bs, 1, n_heads * dh), F32),
        grid_spec=grid_spec,
        compiler_params=_params("arbitrary"),
        name="attn_decode",
    )(page_table, lamv, norm_w, q3, kn3, vn3, ck, cv)
    return out.reshape(bs, n_heads * dh)


def _route(logits, n_groups, n_exp):
    lane = lax.broadcasted_iota(jnp.int32, logits.shape, 1)
    ne = n_groups * n_exp
    big = jnp.int32(LANES)

    def first_max(mask):
        val = jnp.max(jnp.where(mask, logits, NEG), axis=-1, keepdims=True)
        idx = jnp.min(jnp.where(mask & (logits == val), lane, big), axis=-1, keepdims=True)
        return val, idx

    is_group = (lane >= ne) & (lane < ne + n_groups)
    g_max, g_lane = first_max(is_group)
    g_sum = jnp.sum(jnp.where(is_group, jnp.exp(jnp.where(is_group, logits, g_max) - g_max), 0.0),
                    axis=-1, keepdims=True)
    g_w = 1.0 / g_sum
    lo = (g_lane - ne) * n_exp
    in_group = (lane >= lo) & (lane < lo + n_exp)
    v1, i1 = first_max(in_group)
    v2, i2 = first_max(in_group & (lane != i1))
    e = jnp.exp(v2 - v1)
    w1 = g_w / (1.0 + e)
    w2 = g_w * e / (1.0 + e)
    gates = jnp.where(lane == i1, w1, jnp.where(lane == i2, w2, 0.0))
    return jnp.where(lane == ne, (g_lane - ne).astype(F32), gates)


def _out_proj_kernel(oa_ref, ob_ref, x_ref, ing_ref, inb_ref, w_ref, g_ref, b_ref, wr_ref, br_ref,
                     x1_ref, x1b_ref, gate_ref, *, apply_ln, alpha, n_groups, n_exp):
    x = x_ref[...]
    if apply_ln:
        x = _layer_norm(x, ing_ref[...], inb_ref[...])
    wa = oa_ref.shape[1]
    precise = w_ref.dtype == F32
    mix_t = F32 if precise else BF16
    prec = HIGHEST if precise else None
    y = (_dot(oa_ref[...].astype(mix_t), w_ref[:wa, :], prec)
         + _dot(ob_ref[...].astype(mix_t), w_ref[wa:, :], prec))
    x1 = _layer_norm(alpha * x + y, g_ref[...], b_ref[...])
    x1_ref[...] = x1
    x_hi = x1.astype(BF16)
    x1b_ref[...] = x_hi
    x_lo = (x1 - x_hi.astype(F32)).astype(BF16)
    w_r = wr_ref[...]
    w_hi = w_r.astype(BF16)
    w_lo = (w_r - w_hi.astype(F32)).astype(BF16)
    logits = _dot(x_hi, w_hi) + _dot(x_hi, w_lo) + _dot(x_lo, w_hi) + br_ref[...]
    gate_ref[...] = _route(logits, n_groups, n_exp)


def _out_proj(o_a, o_b, x2d, in_g, in_b, w_bf16, g, b, w_router, b_router, *, apply_ln, alpha, n_groups, n_exp):
    n, d = x2d.shape
    tm = _pick_tile(n, 512)
    row = lambda i: (i, 0)
    fixed = lambda i: (0, 0)
    return pl.pallas_call(
        functools.partial(_out_proj_kernel, apply_ln=apply_ln, alpha=alpha, n_groups=n_groups, n_exp=n_exp),
        out_shape=[jax.ShapeDtypeStruct((n, d), F32), jax.ShapeDtypeStruct((n, d), BF16),
                   jax.ShapeDtypeStruct((n, LANES), F32)],
        grid=(n // tm,),
        in_specs=[
            pl.BlockSpec((tm, o_a.shape[1]), row), pl.BlockSpec((tm, o_b.shape[1]), row), pl.BlockSpec((tm, d), row),
            pl.BlockSpec((1, d), fixed), pl.BlockSpec((1, d), fixed), pl.BlockSpec(w_bf16.shape, fixed),
            pl.BlockSpec((1, d), fixed), pl.BlockSpec((1, d), fixed),
            pl.BlockSpec(w_router.shape, fixed), pl.BlockSpec((1, LANES), fixed),
        ],
        out_specs=[pl.BlockSpec((tm, d), row), pl.BlockSpec((tm, d), row), pl.BlockSpec((tm, LANES), row)],
        compiler_params=_params("parallel"),
        name="out_proj",
    )(o_a, o_b, x2d, in_g, in_b, w_bf16, g, b, w_router, b_router)


def _moe_kernel(xb_ref, x1_ref, gate_ref, wg_ref, wu_ref, wd_ref, g_ref, b_ref, o_ref,
                perm_scr, xs_scr, gs_scr, ys_scr, seg_smem, *, alpha, n_groups, parts, row_block):
    j = pl.program_id(1)
    tm, d = x1_ref.shape
    n_exp_part = wg_ref.shape[0]
    lane = lax.broadcasted_iota(jnp.int32, (tm, LANES), 1)

    @pl.when(j == 0)
    def _():
        gates = gate_ref[...]
        gid = gates[:, n_groups * parts * n_exp_part:n_groups * parts * n_exp_part + 1]
        member = jnp.where((lane < n_groups) & (lane.astype(F32) == gid), 1.0, 0.0)
        cum = member
        row = lax.broadcasted_iota(jnp.int32, (tm, LANES), 0)
        step = 1
        while step < tm:
            cum = cum + jnp.where(row >= step, pltpu.roll(cum, step, 0), 0.0)
            step *= 2
        counts = cum[tm - 1:tm, :]
        offs = jnp.zeros_like(counts)
        for k in range(1, n_groups):
            offs = offs + pltpu.roll(counts, k, 1)
        pos = jnp.sum(member * (offs + cum - 1.0), axis=-1, keepdims=True).astype(jnp.int32)
        dest = lax.broadcasted_iota(jnp.int32, (tm, tm), 1)
        perm = jnp.where(dest == pos, 1.0, 0.0).astype(BF16)
        perm_scr[...] = perm
        xs_scr[...] = _dot_tn(perm, xb_ref[...]).astype(BF16)
        gs_scr[...] = sum(_dot_tn(perm, part) for part in _split3(gates))
        ys_scr[...] = jnp.zeros_like(ys_scr)
        lane1 = lax.broadcasted_iota(jnp.int32, (1, LANES), 1)
        for g in range(n_groups):
            seg_smem[0, g] = jnp.sum(jnp.where(lane1 == g, offs, 0.0)).astype(jnp.int32)
            seg_smem[1, g] = jnp.sum(jnp.where(lane1 == g, counts, 0.0)).astype(jnp.int32)

    off = seg_smem[0, j // parts]
    cnt = seg_smem[1, j // parts]
    rb_lane = lax.broadcasted_iota(jnp.int32, (row_block, LANES), 1)
    for lo in range(0, tm, row_block):
        @pl.when((cnt > 0) & (off < lo + row_block) & (off + cnt > lo))
        def _():
            xs = xs_scr[lo:lo + row_block, :]
            gs = gs_scr[lo:lo + row_block, :]
            hg = [_dot(xs, wg_ref[e]) for e in range(n_exp_part)]
            hu = [_dot(xs, wu_ref[e]) for e in range(n_exp_part)]
            acc = None
            for e in range(n_exp_part):
                gate_e = jnp.sum(jnp.where(rb_lane == j * n_exp_part + e, gs, 0.0), axis=-1, keepdims=True)
                y_e = _dot((_silu(hg[e]) * hu[e] * gate_e).astype(BF16), wd_ref[e])
                acc = y_e if acc is None else acc + y_e
            ys_scr[lo:lo + row_block, :] += acc

    @pl.when(j == pl.num_programs(1) - 1)
    def _():
        y = _dot(perm_scr[...], ys_scr[...].astype(BF16))
        o_ref[...] = _layer_norm(alpha * x1_ref[...] + y, g_ref[...], b_ref[...])


def _moe(x1b, x1, gates, wg, wu, wd, g, b, *, alpha, n_groups):
    n, d = x1.shape
    n_exp_total, _, f = wg.shape
    parts = 2
    n_exp_part = n_exp_total // (n_groups * parts)
    tm = _pick_tile(n, 1024)
    row_block = _pick_tile(tm, 256)
    row = lambda i, j: (i, 0)
    fixed = lambda i, j: (0, 0)
    part = lambda i, j: (j, 0, 0, 0)
    shape4 = lambda w: w.reshape(n_groups * parts, n_exp_part, w.shape[1], w.shape[2])
    return pl.pallas_call(
        functools.partial(_moe_kernel, alpha=alpha, n_groups=n_groups, parts=parts, row_block=row_block),
        out_shape=jax.ShapeDtypeStruct((n, d), F32),
        grid=(n // tm, n_groups * parts),
        in_specs=[
            pl.BlockSpec((tm, d), row), pl.BlockSpec((tm, d), row), pl.BlockSpec((tm, LANES), row),
            pl.BlockSpec((None, n_exp_part, d, f), part), pl.BlockSpec((None, n_exp_part, d, f), part),
            pl.BlockSpec((None, n_exp_part, f, d), part),
            pl.BlockSpec((1, d), fixed), pl.BlockSpec((1, d), fixed),
        ],
        out_specs=pl.BlockSpec((tm, d), row),
        scratch_shapes=[pltpu.VMEM((tm, tm), BF16), pltpu.VMEM((tm, d), BF16), pltpu.VMEM((tm, LANES), F32),
                        pltpu.VMEM((tm, d), F32), pltpu.SMEM((2, n_groups), jnp.int32)],
        compiler_params=_params("parallel", "arbitrary"),
        name="moe",
    )(x1b, x1, gates, shape4(wg), shape4(wu), shape4(wd), g, b)


def _pad_lanes(v, offset=0):
    return jnp.zeros((1, LANES), F32).at[0, offset:offset + v.shape[0]].set(v.astype(F32))


def kernel(x_prompt, x_sample, cache_k, cache_v, state_delta, state_conv, page_table, ln_in_g, ln_in_b, w_in, conv_w, a_log, dt_bias, gdn_norm_w, lambda_q1, lambda_k1, lambda_q2, lambda_k2, diff_norm_w, w_out, ln1_g, ln1_b, w_router_group, b_router_group, w_router_expert, b_router_expert, w_gate, w_up, w_down, ln2_g, ln2_b):
    bp, tp, d_model = x_prompt.shape
    bs, ts, _ = x_sample.shape
    depth = w_in.shape[0]
    n_heads_a = a_log.shape[1]
    dv = gdn_norm_w.shape[1]
    c_qkv = conv_w.shape[2]
    kw = conv_w.shape[1]
    c_z = n_heads_a * dv
    dk = (c_qkv - c_z) // (2 * n_heads_a)
    n_heads_b = cache_k.shape[3]
    d_map = lambda_q1.shape[1]
    dh = 2 * d_map
    c_b = n_heads_b * dh
    rope_dim = d_map // 4
    n_groups, n_exp = w_gate.shape[1], w_gate.shape[2]
    d_expert = w_gate.shape[4]
    n_pages = page_table.shape[1]
    past_len = n_pages * cache_k.shape[2]
    alpha = (2.0 * depth) ** 0.25
    q_scale = d_map ** -0.5

    assert ts == 1, "one new token per sample sequence"
    assert dh == LANES and dk == LANES and dv == LANES, "heads are one vreg wide"
    assert math.frexp(q_scale)[0] == 0.5, "score scale must be a power of two to fold into q exactly"
    assert 2 * n_heads_a <= LANES and n_groups * n_exp + n_groups <= LANES
    assert w_in.shape[2] == c_qkv + c_z + 2 * n_heads_a + 3 * c_b

    tabs_p = _rope_tables(jnp.arange(tp), d_map, rope_dim)
    tabs_s = tuple(jnp.tile(t, (bs, 1)) for t in _rope_tables(past_len + jnp.arange(ts), d_map, rope_dim))
    in_g, in_b = ln_in_g.reshape(1, d_model), ln_in_b.reshape(1, d_model)

    xp = x_prompt.reshape(bp * tp, d_model)
    xs = x_sample.reshape(bs * ts, d_model)
    outs = [[] for _ in range(8)]
    for l in range(depth):
        lam_init = 0.8 - 0.6 * math.exp(-0.3 * l)
        first = l == 0
        w = w_in[l]
        o_ba = c_qkv + c_z
        o_b = o_ba + 2 * n_heads_a
        w_ba = jnp.pad(w[:, o_ba:o_b], ((0, 0), (0, LANES - 2 * n_heads_a)))
        w_in_f32 = jnp.concatenate([w[:, :o_ba], w[:, o_b:], w_ba], axis=1)
        w_in_l = w_in_f32.astype(BF16)
        w_out_l = w_out[l].astype(BF16)
        ne = n_groups * n_exp
        w_r = jnp.pad(jnp.concatenate([w_router_expert[l], w_router_group[l]], axis=1),
                      ((0, 0), (0, LANES - ne - n_groups)))
        b_r = _pad_lanes(jnp.concatenate([b_router_expert[l], b_router_group[l]]))
        wg = w_gate[l].reshape(ne, d_model, d_expert).astype(BF16)
        wu = w_up[l].reshape(ne, d_model, d_expert).astype(BF16)
        wd = w_down[l].reshape(ne, d_expert, d_model).astype(BF16)
        alog_pad = _pad_lanes(a_log[l], n_heads_a)
        dt_pad = _pad_lanes(dt_bias[l], n_heads_a)
        gdn_w = gdn_norm_w[l].reshape(1, dv)
        diff_w = diff_norm_w[l].reshape(1, dh)
        lamv = jnp.concatenate([_pad_lanes(v) for v in (lambda_q1[l], lambda_k1[l], lambda_q2[l], lambda_k2[l])]
                               + [jnp.zeros((SUBLANES - 4, LANES), F32)], axis=0)
        ln1 = (ln1_g[l].reshape(1, d_model), ln1_b[l].reshape(1, d_model))
        ln2 = (ln2_g[l].reshape(1, d_model), ln2_b[l].reshape(1, d_model))
        proj = functools.partial(_in_proj, apply_ln=first, c_qkv=c_qkv, c_z=c_z, c_b=c_b, q_scale=q_scale,
                                 rope_half=rope_dim // 2)
        mix_out = functools.partial(_out_proj, apply_ln=first, alpha=alpha, n_groups=n_groups, n_exp=n_exp)
        heads_a = dict(n_heads=n_heads_a, dk=dk, dv=dv)
        heads_b = dict(n_heads=n_heads_b, d_map=d_map, lam_init=lam_init)

        qkv, z, ba, q16, k32, k16, v32, v16 = proj(xp, in_g, in_b, w_in_l, tabs_p)
        shape3 = lambda a: a.reshape(bp, tp, a.shape[1])
        o_a, s_new = _gdn_prompt(shape3(qkv), shape3(z), shape3(ba), conv_w[l], alog_pad, dt_pad, gdn_w, **heads_a)
        o_bp = _attn_prompt(shape3(q16), shape3(k16), shape3(v16), lamv, diff_w, **heads_b)
        x1, x1b, gates = mix_out(o_a.reshape(bp * tp, c_z), o_bp.reshape(bp * tp, c_b), xp, in_g, in_b,
                                 w_out_l, *ln1, w_r, b_r)
        conv_rows = shape3(qkv)[:, tp - (kw - 1):, :]
        if tp < kw - 1:
            conv_rows = jnp.pad(shape3(qkv), ((0, 0), (kw - 1 - tp, 0), (0, 0)))
        xp = _moe(x1b, x1, gates, wg, wu, wd, *ln2, alpha=alpha, n_groups=n_groups)
        for i, a in enumerate((k32.reshape(bp, tp, n_heads_b, dh), v32.reshape(bp, tp, n_heads_b, dh),
                               s_new, conv_rows)):
            outs[i].append(a)

        qkv, z, ba, q16, k32, _, v32, _ = proj(xs, in_g, in_b, w_in_f32, tabs_s)
        o_a, s_new, c_new = _gdn_step(qkv, state_conv[l].reshape(bs, (kw - 1) * c_qkv), z, ba, conv_w[l],
                                      alog_pad, dt_pad, gdn_w, state_delta[l], **heads_a)
        o_bs = _attn_decode(q16, k32, v32, cache_k, cache_v, page_table, l, lamv, diff_w, **heads_b)
        x1, x1b, gates = mix_out(o_a, o_bs, xs, in_g, in_b, w_out[l], *ln1, w_r, b_r)
        xs = _moe(x1b, x1, gates, wg, wu, wd, *ln2, alpha=alpha, n_groups=n_groups)
        for i, a in enumerate((k32.reshape(bs, ts, n_heads_b, dh), v32.reshape(bs, ts, n_heads_b, dh),
                               s_new, c_new.reshape(bs, kw - 1, c_qkv))):
            outs[4 + i].append(a)

    return (xp.reshape(bp, tp, d_model), xs.reshape(bs, ts, d_model)) + tuple(jnp.stack(o) for o in outs)
```

```python
import functools
import math

import jax
import jax.numpy as jnp
from jax import lax
from jax.experimental import pallas as pl
from jax.experimental.pallas import tpu as pltpu

F32 = jnp.float32
BF16 = jnp.bfloat16
HIGHEST = lax.Precision.HIGHEST

LANES = 128
SUBLANES = 8
VMEM_LIMIT_BYTES = 56 * 1024 * 1024

LN_EPS = 1e-5
RMS_EPS = 1e-6
ROPE_THETA = 500000.0
CONV_TAIL = SUBLANES
GDN_CHUNK = 64
GDN_GROUP = 4
DECODE_PAGES = 8
DECODE_SLOTS = 3
ATTN_GROUP = 4
NEG = -1e30


def _pick_tile(n, preferred):
    if n <= preferred:
        return n
    for t in range(preferred, 0, -1):
        if n % t == 0 and t % SUBLANES == 0:
            return t
    return n


def _params(*semantics):
    return pltpu.CompilerParams(dimension_semantics=semantics, vmem_limit_bytes=VMEM_LIMIT_BYTES)


def _dot(a, b, precision=None):
    return jnp.dot(a, b, preferred_element_type=F32, precision=precision)


def _dot_nt(a, b, precision=None):
    return lax.dot_general(a, b, (((1,), (1,)), ((), ())), preferred_element_type=F32, precision=precision)


def _dot_tn(a, b, precision=None):
    return lax.dot_general(a, b, (((0,), (0,)), ((), ())), preferred_element_type=F32, precision=precision)


def _layer_norm(x, g, b):
    mu = jnp.mean(x, axis=-1, keepdims=True)
    xc = x - mu
    var = jnp.mean(xc * xc, axis=-1, keepdims=True)
    return xc * lax.rsqrt(var + LN_EPS) * g + b


def _rms_norm(x, w):
    return x * lax.rsqrt(jnp.mean(x * x, axis=-1, keepdims=True) + RMS_EPS) * w


def _l2_normalize(x):
    return x * lax.rsqrt(jnp.sum(x * x, axis=-1, keepdims=True) + RMS_EPS)


def _silu(x):
    return x * jax.nn.sigmoid(x)


def _softplus(x):
    return jnp.maximum(x, 0.0) + jnp.log1p(jnp.exp(-jnp.abs(x)))


def _lambda(lamv, lam_init):
    s1 = jnp.sum(lamv[0:1] * lamv[1:2], axis=-1, keepdims=True)
    s2 = jnp.sum(lamv[2:3] * lamv[3:4], axis=-1, keepdims=True)
    return jnp.exp(s1) - jnp.exp(s2) + lam_init


def _in_proj_kernel(x_ref, g_ref, b_ref, w_ref, cos_ref, sa_ref, sb_ref,
                    qkv_ref, z_ref, ba_ref, q16_ref, k32_ref, k16_ref, v32_ref, v16_ref,
                    *, apply_ln, c_qkv, c_z, c_b, q_scale, rope_half):
    x = x_ref[...]
    if apply_ln:
        x = _layer_norm(x, g_ref[...], b_ref[...])
    precise = w_ref.dtype == F32
    xin = x if precise else x.astype(BF16)

    def proj(lo, width):
        return _dot(xin, w_ref[:, lo:lo + width], HIGHEST if precise else None)

    qkv_ref[...] = proj(0, c_qkv)
    z_ref[...] = proj(c_qkv, c_z)
    base = c_qkv + c_z
    q = proj(base, c_b)
    k = proj(base + c_b, c_b)
    v = proj(base + 2 * c_b, c_b)
    ba_ref[...] = proj(base + 3 * c_b, LANES)

    cos = cos_ref[...]
    sa = sa_ref[...]
    sb = sb_ref[...]

    def rope(y):
        outs = []
        for h in range(c_b // LANES):
            yh = y[:, h * LANES:(h + 1) * LANES]
            up = pltpu.roll(yh, LANES - rope_half, 1)
            dn = pltpu.roll(yh, rope_half, 1)
            outs.append(yh * cos + up * sa + dn * sb)
        return jnp.concatenate(outs, axis=-1)

    q = rope(q) * q_scale
    k = rope(k)
    q16_ref[...] = q.astype(q16_ref.dtype)
    k16_ref[...] = k.astype(BF16)
    v16_ref[...] = v.astype(BF16)
    tm = x_ref.shape[0]
    n_b = c_b // LANES
    for h in range(n_b):
        k32_ref[pl.ds(h, tm, stride=n_b), :] = k[:, h * LANES:(h + 1) * LANES]
        v32_ref[pl.ds(h, tm, stride=n_b), :] = v[:, h * LANES:(h + 1) * LANES]


def _in_proj(x2d, ln_g, ln_b, w_bf16, tabs, *, apply_ln, c_qkv, c_z, c_b, q_scale, rope_half):
    n, d = x2d.shape
    cos, sa, sb = tabs
    tm = _pick_tile(math.gcd(n, cos.shape[0]), 512)
    tab_tiles = cos.shape[0] // tm
    row = lambda i: (i, 0)
    fixed = lambda i: (0, 0)
    tab = lambda i: (i % tab_tiles, 0)
    n_b = c_b // LANES
    outs = [
        jax.ShapeDtypeStruct((n, c_qkv), F32), jax.ShapeDtypeStruct((n, c_z), F32),
        jax.ShapeDtypeStruct((n, LANES), F32), jax.ShapeDtypeStruct((n, c_b), w_bf16.dtype),
        jax.ShapeDtypeStruct((n * n_b, LANES), F32), jax.ShapeDtypeStruct((n, c_b), BF16),
        jax.ShapeDtypeStruct((n * n_b, LANES), F32), jax.ShapeDtypeStruct((n, c_b), BF16),
    ]
    return pl.pallas_call(
        functools.partial(_in_proj_kernel, apply_ln=apply_ln, c_qkv=c_qkv, c_z=c_z, c_b=c_b,
                          q_scale=q_scale, rope_half=rope_half),
        out_shape=outs,
        grid=(n // tm,),
        in_specs=[
            pl.BlockSpec((tm, d), row), pl.BlockSpec((1, d), fixed), pl.BlockSpec((1, d), fixed),
            pl.BlockSpec(w_bf16.shape, fixed),
            pl.BlockSpec((tm, LANES), tab), pl.BlockSpec((tm, LANES), tab), pl.BlockSpec((tm, LANES), tab),
        ],
        out_specs=[pl.BlockSpec((tm * o.shape[0] // n, o.shape[1]), row) for o in outs],
        compiler_params=_params("parallel"),
        name="in_proj",
    )(x2d, ln_g, ln_b, w_bf16, cos, sa, sb)


def _rope_tables(pos, d_map, rope_dim):
    half = rope_dim // 2
    inv = ROPE_THETA ** (-jnp.arange(0, rope_dim, 2, dtype=F32) / rope_dim)
    ang = pos.astype(F32)[:, None] * inv
    cos, sin = jnp.cos(ang), jnp.sin(ang)
    n = pos.shape[0]
    ones = jnp.ones((n, d_map - rope_dim), F32)
    zeros_rest = jnp.zeros((n, d_map - rope_dim), F32)
    zeros_half = jnp.zeros((n, half), F32)
    cos_m = jnp.concatenate([cos, cos, ones], axis=1)
    sa_m = jnp.concatenate([-sin, zeros_half, zeros_rest], axis=1)
    sb_m = jnp.concatenate([zeros_half, sin, zeros_rest], axis=1)
    return tuple(jnp.concatenate([t, t], axis=1) for t in (cos_m, sa_m, sb_m))


def _split3(x):
    x1 = x.astype(BF16)
    r1 = x - x1.astype(F32)
    x2 = r1.astype(BF16)
    x3 = (r1 - x2.astype(F32)).astype(BF16)
    return x1, x2, x3


def _neumann_series(ms):
    n = ms[0].shape[0]
    accs = list(ms)
    qs = [_dot(m.astype(BF16), m.astype(BF16)) for m in ms]
    span = 4
    while span < n:
        rs = [_dot(jnp.concatenate([a.astype(BF16), q.astype(BF16)], axis=0), q.astype(BF16))
              for a, q in zip(accs, qs)]
        accs = [a + q + r[:n] for a, q, r in zip(accs, qs, rs)]
        qs = [r[n:] for r in rs]
        span *= 2
    return [a + q + _dot(a.astype(BF16), q.astype(BF16)) for a, q in zip(accs, qs)]


def _gdn_prompt_kernel(qkv_ref, z_ref, ba_ref, cw_ref, alog_ref, dt_ref, nw_ref,
                       o_ref, s_ref, y_scr, gam_scr, beta_scr, prev_scr, state_scr,
                       *, n_heads, dk, dv, chunk, group):
    t = pl.program_id(1)
    tt, c = qkv_ref.shape

    @pl.when(t == 0)
    def _():
        prev_scr[...] = jnp.zeros_like(prev_scr)
        state_scr[...] = jnp.zeros_like(state_scr)

    x = qkv_ref[...]
    prev = prev_scr[...]
    cw = cw_ref[...]
    kw = cw.shape[0]
    tail_row = lax.broadcasted_iota(jnp.int32, (CONV_TAIL, c), 0)
    acc = x * cw[kw - 1:kw, :]
    for j in range(1, kw):
        xs = pltpu.roll(x, j, 0)
        ps = pltpu.roll(prev, j, 0)
        head = jnp.where(tail_row < j, ps, xs[0:CONV_TAIL])
        xs = jnp.concatenate([head, xs[CONV_TAIL:]], axis=0)
        acc = acc + xs * cw[kw - 1 - j:kw - j, :]
    prev_scr[...] = x[tt - CONV_TAIL:tt]
    y_scr[...] = _silu(acc)

    ba = ba_ref[...]
    beta_scr[...] = jax.nn.sigmoid(ba)
    gam = -jnp.exp(alog_ref[...]) * _softplus(ba + dt_ref[...])
    row_in_chunk = lax.broadcasted_iota(jnp.int32, (tt, LANES), 0) % chunk
    step = 1
    while step < chunk:
        gam = gam + jnp.where(row_in_chunk >= step, pltpu.roll(gam, step, 0), 0.0)
        step *= 2
    gam_scr[...] = gam

    ri = lax.broadcasted_iota(jnp.int32, (chunk, chunk), 0)
    ci = lax.broadcasted_iota(jnp.int32, (chunk, chunk), 1)
    incl = ri >= ci
    strict = ri > ci
    sel_r = lax.broadcasted_iota(jnp.int32, (SUBLANES, LANES), 0)
    sel_c = lax.broadcasted_iota(jnp.int32, (SUBLANES, LANES), 1)
    head_rows = (sel_c == sel_r + n_heads).astype(BF16)
    nw = nw_ref[...]

    def chunk_group(ig, carry):
        starts = [pl.multiple_of((ig * group + c) * chunk, chunk) for c in range(group)]
        units = []
        for r0 in starts:
            gam = gam_scr[pl.ds(r0, chunk), :]
            b_c = beta_scr[pl.ds(r0, chunk), :]
            y_c = y_scr[pl.ds(r0, chunk), :]
            picked = _dot_nt(head_rows, jnp.concatenate(_split3(gam), axis=0))
            gam_rows = picked[:, :chunk] + picked[:, chunk:2 * chunk] + picked[:, 2 * chunk:]
            e_gam = jnp.exp(gam)
            g_last = gam[chunk - 1:chunk, :]
            k_end_scale = jnp.exp(g_last - gam)
            g_end = jnp.exp(g_last)
            for h in range(n_heads):
                gl = n_heads + h
                qh = _l2_normalize(y_c[:, h * dk:(h + 1) * dk]) * (dk ** -0.5)
                kh = _l2_normalize(y_c[:, (n_heads + h) * dk:(n_heads + h + 1) * dk])
                vh = y_c[:, 2 * n_heads * dk + h * dv:2 * n_heads * dk + (h + 1) * dv]
                diff = gam[:, gl:gl + 1] - gam_rows[h:h + 1, :]
                units.append(dict(
                    qh=qh, kh=kh, vh=vh, beta=b_c[:, h:h + 1], eg=e_gam[:, gl:gl + 1],
                    decay=jnp.where(incl, jnp.exp(jnp.where(incl, diff, 0.0)), 0.0),
                    k_end=k_end_scale[:, gl:gl + 1] * kh, g_end=g_end[:, gl:gl + 1]))
        qk_kk = [_dot_nt(jnp.concatenate([u["qh"], u["kh"]], axis=0).astype(BF16), u["kh"].astype(BF16))
                 for u in units]
        attn = [x[:chunk] * u["decay"] for x, u in zip(qk_kk, units)]
        ms = [jnp.where(strict, -(u["beta"] * x[chunk:] * u["decay"]), 0.0) for x, u in zip(qk_kk, units)]
        inv_minus_i = _neumann_series(ms)
        rhs = [jnp.concatenate([u["beta"] * u["vh"], (u["beta"] * u["eg"]) * u["kh"]], axis=-1) for u in units]
        sols = [r + _dot(n.astype(BF16), r.astype(BF16)) for n, r in zip(inv_minus_i, rhs)]

        states = [state_scr[h] for h in range(n_heads)]
        for c, r0 in enumerate(starts):
            ids = range(c * n_heads, (c + 1) * n_heads)
            wq = [_dot(jnp.concatenate([sols[i][:, dv:], units[i]["eg"] * units[i]["qh"]], axis=0).astype(BF16),
                       s.astype(BF16)) for i, s in zip(ids, states)]
            ub = [(sols[i][:, :dv] - x[:chunk]).astype(BF16) for i, x in zip(ids, wq)]
            o = [x[chunk:] + _dot(attn[i].astype(BF16), u) for i, x, u in zip(ids, wq, ub)]
            states = [units[i]["g_end"] * s + _dot_tn(units[i]["k_end"].astype(BF16), u)
                      for i, s, u in zip(ids, states, ub)]
            z_c = z_ref[pl.ds(r0, chunk), :]
            o_ref[pl.ds(r0, chunk), :] = jnp.concatenate(
                [_rms_norm(x, nw) * _silu(z_c[:, h * dv:(h + 1) * dv]) for h, x in enumerate(o)], axis=-1)
        for h in range(n_heads):
            state_scr[h] = states[h]
        return carry

    lax.fori_loop(0, tt // (group * chunk), chunk_group, 0)

    @pl.when(t == pl.num_programs(1) - 1)
    def _():
        s_ref[...] = state_scr[...]


def _gdn_prompt(qkv, z, ba, conv_w, alog_pad, dt_pad, norm_w, *, n_heads, dk, dv):
    b, t, c = qkv.shape
    tt = _pick_tile(t, 512)
    group = math.gcd(tt // GDN_CHUNK, GDN_GROUP)
    assert tt % (group * GDN_CHUNK) == 0 and tt >= CONV_TAIL
    tile = lambda ib, it: (ib, it, 0)
    fixed = lambda ib, it: (0, 0)
    return pl.pallas_call(
        functools.partial(_gdn_prompt_kernel, n_heads=n_heads, dk=dk, dv=dv, chunk=GDN_CHUNK, group=group),
        out_shape=[jax.ShapeDtypeStruct((b, t, n_heads * dv), F32),
                   jax.ShapeDtypeStruct((b, n_heads, dk, dv), F32)],
        grid=(b, t // tt),
        in_specs=[
            pl.BlockSpec((None, tt, c), tile), pl.BlockSpec((None, tt, n_heads * dv), tile),
            pl.BlockSpec((None, tt, LANES), tile),
            pl.BlockSpec(conv_w.shape, fixed), pl.BlockSpec((1, LANES), fixed), pl.BlockSpec((1, LANES), fixed),
            pl.BlockSpec((1, dv), fixed),
        ],
        out_specs=[pl.BlockSpec((None, tt, n_heads * dv), tile),
                   pl.BlockSpec((None, n_heads, dk, dv), lambda ib, it: (ib, 0, 0, 0))],
        scratch_shapes=[pltpu.VMEM((tt, c), F32), pltpu.VMEM((tt, LANES), F32), pltpu.VMEM((tt, LANES), F32),
                        pltpu.VMEM((CONV_TAIL, c), F32), pltpu.VMEM((n_heads, dk, dv), F32)],
        compiler_params=_params("parallel", "arbitrary"),
        name="gdn_prompt",
    )(qkv, z, ba, conv_w, alog_pad, dt_pad, norm_w)


def _gdn_step_kernel(qkv_ref, sc_ref, z_ref, ba_ref, cw_ref, alog_ref, dt_ref, nw_ref, s_ref,
                     o_ref, sn_ref, cn_ref, *, n_heads, dk, dv):
    nb, c = qkv_ref.shape
    x = qkv_ref[...]
    sc = sc_ref[...]
    cw = cw_ref[...]
    kw = cw.shape[0]
    acc = x * cw[kw - 1:kw, :]
    for i in range(kw - 1):
        acc = acc + sc[:, i * c:(i + 1) * c] * cw[i:i + 1, :]
    y = _silu(acc)
    cn_ref[...] = jnp.concatenate([sc[:, c:], x], axis=-1)

    ba = ba_ref[...]
    beta = jax.nn.sigmoid(ba)
    e_g = jnp.exp(-jnp.exp(alog_ref[...]) * _softplus(ba + dt_ref[...]))
    nw = nw_ref[...]
    pad_k = jnp.zeros((SUBLANES - 2, dk), F32)
    pad_1k = jnp.zeros((SUBLANES - 1, dk), F32)
    pad_1v = jnp.zeros((SUBLANES - 1, dv), F32)
    outs = []
    for h in range(n_heads):
        gl = n_heads + h
        qh = _l2_normalize(y[:, h * dk:(h + 1) * dk]) * (dk ** -0.5)
        kh = _l2_normalize(y[:, (n_heads + h) * dk:(n_heads + h + 1) * dk])
        vh = y[:, 2 * n_heads * dk + h * dv:2 * n_heads * dk + (h + 1) * dv]
        beta_h = beta[:, h:h + 1]
        eg_h = e_g[:, gl:gl + 1]
        qk = jnp.sum(qh * kh, axis=-1, keepdims=True)
        rows = []
        for ib in range(nb):
            s = s_ref[ib, h]
            k_row = kh[ib:ib + 1]
            lhs = jnp.concatenate([k_row, qh[ib:ib + 1], pad_k], axis=0)
            kq = _dot(lhs, s, HIGHEST)
            b_i = beta_h[ib:ib + 1]
            e_i = eg_h[ib:ib + 1]
            u = b_i * vh[ib:ib + 1] - (b_i * e_i) * kq[0:1]
            rows.append(e_i * kq[1:2] + qk[ib:ib + 1] * u)
            outer = _dot_tn(jnp.concatenate([k_row, pad_1k], axis=0), jnp.concatenate([u, pad_1v], axis=0),
                            HIGHEST)
            sn_ref[ib, h] = e_i * s + outer
        o = jnp.concatenate(rows, axis=0)
        outs.append(_rms_norm(o, nw) * _silu(z_ref[:, h * dv:(h + 1) * dv]))
    o_ref[...] = jnp.concatenate(outs, axis=-1)


def _gdn_step(qkv, conv_state, z, ba, conv_w, alog_pad, dt_pad, norm_w, state, *, n_heads, dk, dv):
    n, c = qkv.shape
    nb = _pick_tile(n, SUBLANES)
    row = lambda i: (i, 0)
    fixed = lambda i: (0, 0)
    st = lambda i: (i, 0, 0, 0)
    return pl.pallas_call(
        functools.partial(_gdn_step_kernel, n_heads=n_heads, dk=dk, dv=dv),
        out_shape=[jax.ShapeDtypeStruct((n, n_heads * dv), F32), jax.ShapeDtypeStruct(state.shape, F32),
                   jax.ShapeDtypeStruct(conv_state.shape, F32)],
        grid=(n // nb,),
        in_specs=[
            pl.BlockSpec((nb, c), row), pl.BlockSpec((nb, conv_state.shape[1]), row),
            pl.BlockSpec((nb, n_heads * dv), row), pl.BlockSpec((nb, LANES), row),
            pl.BlockSpec(conv_w.shape, fixed), pl.BlockSpec((1, LANES), fixed), pl.BlockSpec((1, LANES), fixed),
            pl.BlockSpec((1, dv), fixed), pl.BlockSpec((nb, n_heads, dk, dv), st),
        ],
        out_specs=[pl.BlockSpec((nb, n_heads * dv), row), pl.BlockSpec((nb, n_heads, dk, dv), st),
                   pl.BlockSpec((nb, conv_state.shape[1]), row)],
        compiler_params=_params("parallel"),
        name="gdn_step",
    )(qkv, conv_state, z, ba, conv_w, alog_pad, dt_pad, norm_w, state)


def _softmax_step(s, v_bf16, m, l, acc):
    m_new = jnp.maximum(m, jnp.max(s, axis=-1, keepdims=True))
    alpha = jnp.exp(m - m_new)
    p = jnp.exp(s - m_new)
    l = alpha * l + jnp.sum(p, axis=-1, keepdims=True)
    acc = alpha * acc + _dot(p.astype(BF16), v_bf16)
    return m_new, l, acc


def _attn_prompt_kernel(lamv_ref, nw_ref, q_ref, k_ref, v_ref, o_ref, *, tk, d_map, lam_init):
    iq = pl.program_id(2)
    tq, dh = q_ref.shape
    q = q_ref[...]
    lane = lax.broadcasted_iota(jnp.int32, (tq, dh), 1)
    zero = jnp.zeros_like(q)
    q2 = jnp.concatenate([jnp.where(lane < d_map, q, zero), jnp.where(lane >= d_map, q, zero)], axis=0)
    rows = 2 * tq
    n_full = (iq * tq) // tk

    def scores(j):
        return _dot_nt(q2, k_ref[pl.ds(pl.multiple_of(j * tk, tk), tk), :])

    def values(j):
        return v_ref[pl.ds(pl.multiple_of(j * tk, tk), tk), :]

    def block_group(jg, carry):
        s = [scores(ATTN_GROUP * jg + i) for i in range(ATTN_GROUP)]
        for i in range(ATTN_GROUP):
            carry = _softmax_step(s[i], values(ATTN_GROUP * jg + i), *carry)
        return carry

    def single_block(j, carry):
        return _softmax_step(scores(j), values(j), *carry)

    carry = (jnp.full((rows, 1), NEG, F32), jnp.zeros((rows, 1), F32), jnp.zeros((rows, dh), F32))
    carry = lax.fori_loop(0, n_full // ATTN_GROUP, block_group, carry)
    carry = lax.fori_loop(n_full - n_full % ATTN_GROUP, n_full, single_block, carry)

    r = lax.broadcasted_iota(jnp.int32, (rows, tk), 0)
    col_minus_row = lax.broadcasted_iota(jnp.int32, (rows, tk), 1) - jnp.where(r < tq, r, r - tq)
    s = jnp.where(col_minus_row <= iq * tq - n_full * tk, scores(n_full), NEG)
    _, l, acc = _softmax_step(s, values(n_full), *carry)
    on = acc / l
    lam = _lambda(lamv_ref[...], lam_init)
    o = on[:tq] - lam * on[tq:]
    o_ref[...] = _rms_norm(o, nw_ref[...]) * (1.0 - lam_init)


def _attn_prompt(q16, k16, v16, lamv, norm_w, *, n_heads, d_map, lam_init):
    b, t, _ = q16.shape
    dh = 2 * d_map
    tq = _pick_tile(t, 512)
    tk = _pick_tile(t, 512)
    assert tk % tq == 0, "one key block must cover a query tile's diagonal"
    fixed = lambda ib, ih, iq: (0, 0)
    return pl.pallas_call(
        functools.partial(_attn_prompt_kernel, tk=tk, d_map=d_map, lam_init=lam_init),
        out_shape=jax.ShapeDtypeStruct((b, t, n_heads * dh), F32),
        grid=(b, n_heads, t // tq),
        in_specs=[
            pl.BlockSpec(lamv.shape, fixed), pl.BlockSpec((1, dh), fixed),
            pl.BlockSpec((None, tq, dh), lambda ib, ih, iq: (ib, iq, ih)),
            pl.BlockSpec((None, t, dh), lambda ib, ih, iq: (ib, 0, ih)),
            pl.BlockSpec((None, t, dh), lambda ib, ih, iq: (ib, 0, ih)),
        ],
        out_specs=pl.BlockSpec((None, tq, dh), lambda ib, ih, iq: (ib, iq, ih)),
        compiler_params=_params("parallel", "parallel", "arbitrary"),
        name="attn_prompt",
    )(lamv, norm_w, q16, k16, v16)


def _attn_decode_kernel(pt_ref, lamv_ref, nw_ref, q_ref, kn_ref, vn_ref, ck_ref, cv_ref, o_ref,
                        kbuf, vbuf, sems, *, pages, page_base, n_heads, d_map, lam_init):
    ib = pl.program_id(0)
    n_seq = pl.num_programs(0)
    n_chunks = pt_ref.shape[1] // pages
    dh = 2 * d_map
    page_rows = kbuf.shape[1] // pages
    rows = 2 * SUBLANES

    def page_copies(seq, chunk, slot):
        copies = []
        for p in range(pages):
            page = page_base + pt_ref[seq, chunk * pages + p]
            dst = pl.ds(p * page_rows, page_rows)
            copies.append(pltpu.make_async_copy(ck_ref.at[page], kbuf.at[slot, dst, :], sems.at[slot, 0]))
            copies.append(pltpu.make_async_copy(cv_ref.at[page], vbuf.at[slot, dst, :], sems.at[slot, 1]))
        return copies

    def head_rows(x, mask_maps):
        lane = lax.broadcasted_iota(jnp.int32, (1, dh), 1)
        out = []
        for m in range(2):
            for h in range(n_heads):
                xh = x[:, h * dh:(h + 1) * dh]
                if mask_maps:
                    xh = jnp.where((lane >= m * d_map) & (lane < (m + 1) * d_map), xh, 0.0)
                out.append(xh)
        out.append(jnp.zeros((rows - 2 * n_heads, dh), F32))
        return jnp.concatenate(out, axis=0)

    def two_term_dot(dot, a, b):
        a_hi = a.astype(BF16)
        a_lo = (a - a_hi.astype(F32)).astype(BF16)
        b_hi = b.astype(BF16)
        b_lo = (b - b_hi.astype(F32)).astype(BF16)
        both = dot(jnp.concatenate([a_hi, a_lo], axis=0), b_hi)
        return both[:rows] + both[rows:] + dot(a_hi, b_lo)

    n_slots = kbuf.shape[0]
    ahead = n_slots - 1

    def start_chunk(g_seq, g_chunk, slot):
        seq = g_seq + g_chunk // n_chunks

        @pl.when(seq < n_seq)
        def _():
            for cp in page_copies(seq, g_chunk % n_chunks, slot):
                cp.start()

    @pl.when(ib == 0)
    def _():
        for g in range(ahead):
            start_chunk(0, g, g % n_slots)

    qmat = head_rows(q_ref[...].astype(F32), True)
    n_cols = pages * page_rows
    own_head = (lax.broadcasted_iota(jnp.int32, (rows, n_cols), 1) % n_heads
                == lax.broadcasted_iota(jnp.int32, (rows, n_cols), 0) % n_heads)

    def chunk_body(ic, carry):
        m_old, l_old, acc_old = carry
        g = ib * n_chunks + ic
        slot = g % n_slots
        start_chunk(ib, ic + ahead, (g + ahead) % n_slots)
        for cp in page_copies(ib, ic, slot):
            cp.wait()
        s = jnp.where(own_head, two_term_dot(_dot_nt, qmat, kbuf[slot]), NEG)
        m_new = jnp.maximum(m_old, jnp.max(s, axis=-1, keepdims=True))
        alpha = jnp.exp(m_old - m_new)
        p = jnp.exp(s - m_new)
        l_new = alpha * l_old + jnp.sum(p, axis=-1, keepdims=True)
        acc_new = alpha * acc_old + two_term_dot(_dot, p, vbuf[slot])
        return m_new, l_new, acc_new

    init = (jnp.full((rows, 1), NEG, F32), jnp.zeros((rows, 1), F32), jnp.zeros((rows, dh), F32))
    m_c, l_c, acc_c = lax.fori_loop(0, n_chunks, chunk_body, init)

    s_new = jnp.sum(qmat * head_rows(kn_ref[...], False), axis=-1, keepdims=True)
    m_f = jnp.maximum(m_c, s_new)
    a = jnp.exp(m_c - m_f)
    p_new = jnp.exp(s_new - m_f)
    l_f = a * l_c + p_new
    acc_f = a * acc_c + p_new * head_rows(vn_ref[...], False)
    on = acc_f / l_f
    lam = _lambda(lamv_ref[...], lam_init)
    o = on[0:n_heads] - lam * on[n_heads:2 * n_heads]
    o = _rms_norm(o, nw_ref[...]) * (1.0 - lam_init)
    o_ref[...] = jnp.concatenate([o[h:h + 1] for h in range(n_heads)], axis=-1)


def _attn_decode(q, k_new, v_new, cache_k, cache_v, page_table, layer, lamv, norm_w,
                 *, n_heads, d_map, lam_init):
    bs = q.shape[0]
    dh = 2 * d_map
    depth, n_pool, page, _, _ = cache_k.shape
    n_pages = page_table.shape[1]
    pages = math.gcd(n_pages, DECODE_PAGES)
    page_rows = page * n_heads
    assert 2 * SUBLANES >= 2 * n_heads
    ck = cache_k.reshape(depth * n_pool, page_rows, dh)
    cv = cache_v.reshape(depth * n_pool, page_rows, dh)
    q3 = q.reshape(bs, 1, n_heads * dh)
    kn3 = k_new.reshape(bs, 1, n_heads * dh)
    vn3 = v_new.reshape(bs, 1, n_heads * dh)
    fixed = lambda ib, pt: (0, 0)
    seq = lambda ib, pt: (ib, 0, 0)
    grid_spec = pltpu.PrefetchScalarGridSpec(
        num_scalar_prefetch=1,
        grid=(bs,),
        in_specs=[pl.BlockSpec(lamv.shape, fixed), pl.BlockSpec((1, dh), fixed),
                  pl.BlockSpec((None, 1, n_heads * dh), seq), pl.BlockSpec((None, 1, n_heads * dh), seq),
                  pl.BlockSpec((None, 1, n_heads * dh), seq),
                  pl.BlockSpec(memory_space=pl.ANY), pl.BlockSpec(memory_space=pl.ANY)],
        out_specs=pl.BlockSpec((None, 1, n_heads * dh), seq),
        scratch_shapes=[pltpu.VMEM((DECODE_SLOTS, pages * page_rows, dh), F32),
                        pltpu.VMEM((DECODE_SLOTS, pages * page_rows, dh), F32),
                        pltpu.SemaphoreType.DMA((DECODE_SLOTS, 2))],
    )
    out = pl.pallas_call(
        functools.partial(_attn_decode_kernel, pages=pages, page_base=layer * n_pool, n_heads=n_heads,
                          d_map=d_map, lam_init=lam_init),
        out_shape=jax.ShapeDtypeStruct((bs, 1, n_heads * dh), F32),
        grid_spec=grid_spec,
        compiler_params=_params("arbitrary"),
        name="attn_decode",
    )(page_table, lamv, norm_w, q3, kn3, vn3, ck, cv)
    return out.reshape(bs, n_heads * dh)


def _route(logits, n_groups, n_exp):
    lane = lax.broadcasted_iota(jnp.int32, logits.shape, 1)
    ne = n_groups * n_exp
    big = jnp.int32(LANES)

    def first_max(mask):
        val = jnp.max(jnp.where(mask, logits, NEG), axis=-1, keepdims=True)
        idx = jnp.min(jnp.where(mask & (logits == val), lane, big), axis=-1, keepdims=True)
        return val, idx

    is_group = (lane >= ne) & (lane < ne + n_groups)
    g_max, g_lane = first_max(is_group)
    g_sum = jnp.sum(jnp.where(is_group, jnp.exp(jnp.where(is_group, logits, g_max) - g_max), 0.0),
                    axis=-1, keepdims=True)
    g_w = 1.0 / g_sum
    lo = (g_lane - ne) * n_exp
    in_group = (lane >= lo) & (lane < lo + n_exp)
    v1, i1 = first_max(in_group)
    v2, i2 = first_max(in_group & (lane != i1))
    e = jnp.exp(v2 - v1)
    w1 = g_w / (1.0 + e)
    w2 = g_w * e / (1.0 + e)
    gates = jnp.where(lane == i1, w1, jnp.where(lane == i2, w2, 0.0))
    return jnp.where(lane == ne, (g_lane - ne).astype(F32), gates)


def _out_proj_kernel(oa_ref, ob_ref, x_ref, ing_ref, inb_ref, w_ref, g_ref, b_ref, wr_ref, br_ref,
                     x1_ref, x1b_ref, gate_ref, *, apply_ln, alpha, n_groups, n_exp):
    x = x_ref[...]
    if apply_ln:
        x = _layer_norm(x, ing_ref[...], inb_ref[...])
    wa = oa_ref.shape[1]
    precise = w_ref.dtype == F32
    mix_t = F32 if precise else BF16
    prec = HIGHEST if precise else None
    y = (_dot(oa_ref[...].astype(mix_t), w_ref[:wa, :], prec)
         + _dot(ob_ref[...].astype(mix_t), w_ref[wa:, :], prec))
    x1 = _layer_norm(alpha * x + y, g_ref[...], b_ref[...])
    x1_ref[...] = x1
    x_hi = x1.astype(BF16)
    x1b_ref[...] = x_hi
    x_lo = (x1 - x_hi.astype(F32)).astype(BF16)
    w_r = wr_ref[...]
    w_hi = w_r.astype(BF16)
    w_lo = (w_r - w_hi.astype(F32)).astype(BF16)
    logits = _dot(x_hi, w_hi) + _dot(x_hi, w_lo) + _dot(x_lo, w_hi) + br_ref[...]
    gate_ref[...] = _route(logits, n_groups, n_exp)


def _out_proj(o_a, o_b, x2d, in_g, in_b, w_bf16, g, b, w_router, b_router, *, apply_ln, alpha, n_groups, n_exp):
    n, d = x2d.shape
    tm = _pick_tile(n, 512)
    row = lambda i: (i, 0)
    fixed = lambda i: (0, 0)
    return pl.pallas_call(
        functools.partial(_out_proj_kernel, apply_ln=apply_ln, alpha=alpha, n_groups=n_groups, n_exp=n_exp),
        out_shape=[jax.ShapeDtypeStruct((n, d), F32), jax.ShapeDtypeStruct((n, d), BF16),
                   jax.ShapeDtypeStruct((n, LANES), F32)],
        grid=(n // tm,),
        in_specs=[
            pl.BlockSpec((tm, o_a.shape[1]), row), pl.BlockSpec((tm, o_b.shape[1]), row), pl.BlockSpec((tm, d), row),
            pl.BlockSpec((1, d), fixed), pl.BlockSpec((1, d), fixed), pl.BlockSpec(w_bf16.shape, fixed),
            pl.BlockSpec((1, d), fixed), pl.BlockSpec((1, d), fixed),
            pl.BlockSpec(w_router.shape, fixed), pl.BlockSpec((1, LANES), fixed),
        ],
        out_specs=[pl.BlockSpec((tm, d), row), pl.BlockSpec((tm, d), row), pl.BlockSpec((tm, LANES), row)],
        compiler_params=_params("parallel"),
        name="out_proj",
    )(o_a, o_b, x2d, in_g, in_b, w_bf16, g, b, w_router, b_router)


def _moe_kernel(xb_ref, x1_ref, gate_ref, wg_ref, wu_ref, wd_ref, g_ref, b_ref, o_ref,
                perm_scr, xs_scr, gs_scr, ys_scr, seg_smem, *, alpha, n_groups, parts, row_block):
    j = pl.program_id(1)
    tm, d = x1_ref.shape
    n_exp_part = wg_ref.shape[0]
    lane = lax.broadcasted_iota(jnp.int32, (tm, LANES), 1)

    @pl.when(j == 0)
    def _():
        gates = gate_ref[...]
        gid = gates[:, n_groups * parts * n_exp_part:n_groups * parts * n_exp_part + 1]
        member = jnp.where((lane < n_groups) & (lane.astype(F32) == gid), 1.0, 0.0)
        cum = member
        row = lax.broadcasted_iota(jnp.int32, (tm, LANES), 0)
        step = 1
        while step < tm:
            cum = cum + jnp.where(row >= step, pltpu.roll(cum, step, 0), 0.0)
            step *= 2
        counts = cum[tm - 1:tm, :]
        offs = jnp.zeros_like(counts)
        for k in range(1, n_groups):
            offs = offs + pltpu.roll(counts, k, 1)
        pos = jnp.sum(member * (offs + cum - 1.0), axis=-1, keepdims=True).astype(jnp.int32)
        dest = lax.broadcasted_iota(jnp.int32, (tm, tm), 1)
        perm = jnp.where(dest == pos, 1.0, 0.0).astype(BF16)
        perm_scr[...] = perm
        xs_scr[...] = _dot_tn(perm, xb_ref[...]).astype(BF16)
        gs_scr[...] = sum(_dot_tn(perm, part) for part in _split3(gates))
        ys_scr[...] = jnp.zeros_like(ys_scr)
        lane1 = lax.broadcasted_iota(jnp.int32, (1, LANES), 1)
        for g in range(n_groups):
            seg_smem[0, g] = jnp.sum(jnp.where(lane1 == g, offs, 0.0)).astype(jnp.int32)
            seg_smem[1, g] = jnp.sum(jnp.where(lane1 == g, counts, 0.0)).astype(jnp.int32)

    off = seg_smem[0, j // parts]
    cnt = seg_smem[1, j // parts]
    rb_lane = lax.broadcasted_iota(jnp.int32, (row_block, LANES), 1)
    for lo in range(0, tm, row_block):
        @pl.when((cnt > 0) & (off < lo + row_block) & (off + cnt > lo))
        def _():
            xs = xs_scr[lo:lo + row_block, :]
            gs = gs_scr[lo:lo + row_block, :]
            hg = [_dot(xs, wg_ref[e]) for e in range(n_exp_part)]
            hu = [_dot(xs, wu_ref[e]) for e in range(n_exp_part)]
            acc = None
            for e in range(n_exp_part):
                gate_e = jnp.sum(jnp.where(rb_lane == j * n_exp_part + e, gs, 0.0), axis=-1, keepdims=True)
                y_e = _dot((_silu(hg[e]) * hu[e] * gate_e).astype(BF16), wd_ref[e])
                acc = y_e if acc is None else acc + y_e
            ys_scr[lo:lo + row_block, :] += acc

    @pl.when(j == pl.num_programs(1) - 1)
    def _():
        y = _dot(perm_scr[...], ys_scr[...].astype(BF16))
        o_ref[...] = _layer_norm(alpha * x1_ref[...] + y, g_ref[...], b_ref[...])


def _moe(x1b, x1, gates, wg, wu, wd, g, b, *, alpha, n_groups):
    n, d = x1.shape
    n_exp_total, _, f = wg.shape
    parts = 2
    n_exp_part = n_exp_total // (n_groups * parts)
    tm = _pick_tile(n, 1024)
    row_block = _pick_tile(tm, 256)
    row = lambda i, j: (i, 0)
    fixed = lambda i, j: (0, 0)
    part = lambda i, j: (j, 0, 0, 0)
    shape4 = lambda w: w.reshape(n_groups * parts, n_exp_part, w.shape[1], w.shape[2])
    return pl.pallas_call(
        functools.partial(_moe_kernel, alpha=alpha, n_groups=n_groups, parts=parts, row_block=row_block),
        out_shape=jax.ShapeDtypeStruct((n, d), F32),
        grid=(n // tm, n_groups * parts),
        in_specs=[
            pl.BlockSpec((tm, d), row), pl.BlockSpec((tm, d), row), pl.BlockSpec((tm, LANES), row),
            pl.BlockSpec((None, n_exp_part, d, f), part), pl.BlockSpec((None, n_exp_part, d, f), part),
            pl.BlockSpec((None, n_exp_part, f, d), part),
            pl.BlockSpec((1, d), fixed), pl.BlockSpec((1, d), fixed),
        ],
        out_specs=pl.BlockSpec((tm, d), row),
        scratch_shapes=[pltpu.VMEM((tm, tm), BF16), pltpu.VMEM((tm, d), BF16), pltpu.VMEM((tm, LANES), F32),
                        pltpu.VMEM((tm, d), F32), pltpu.SMEM((2, n_groups), jnp.int32)],
        compiler_params=_params("parallel", "arbitrary"),
        name="moe",
    )(x1b, x1, gates, shape4(wg), shape4(wu), shape4(wd), g, b)


def _pad_lanes(v, offset=0):
    return jnp.zeros((1, LANES), F32).at[0, offset:offset + v.shape[0]].set(v.astype(F32))


def kernel(x_prompt, x_sample, cache_k, cache_v, state_delta, state_conv, page_table, ln_in_g, ln_in_b, w_in, conv_w, a_log, dt_bias, gdn_norm_w, lambda_q1, lambda_k1, lambda_q2, lambda_k2, diff_norm_w, w_out, ln1_g, ln1_b, w_router_group, b_router_group, w_router_expert, b_router_expert, w_gate, w_up, w_down, ln2_g, ln2_b):
    bp, tp, d_model = x_prompt.shape
    bs, ts, _ = x_sample.shape
    depth = w_in.shape[0]
    n_heads_a = a_log.shape[1]
    dv = gdn_norm_w.shape[1]
    c_qkv = conv_w.shape[2]
    kw = conv_w.shape[1]
    c_z = n_heads_a * dv
    dk = (c_qkv - c_z) // (2 * n_heads_a)
    n_heads_b = cache_k.shape[3]
    d_map = lambda_q1.shape[1]
    dh = 2 * d_map
    c_b = n_heads_b * dh
    rope_dim = d_map // 4
    n_groups, n_exp = w_gate.shape[1], w_gate.shape[2]
    d_expert = w_gate.shape[4]
    n_pages = page_table.shape[1]
    past_len = n_pages * cache_k.shape[2]
    alpha = (2.0 * depth) ** 0.25
    q_scale = d_map ** -0.5

    assert ts == 1, "one new token per sample sequence"
    assert dh == LANES and dk == LANES and dv == LANES, "heads are one vreg wide"
    assert math.frexp(q_scale)[0] == 0.5, "score scale must be a power of two to fold into q exactly"
    assert 2 * n_heads_a <= LANES and n_groups * n_exp + n_groups <= LANES
    assert w_in.shape[2] == c_qkv + c_z + 2 * n_heads_a + 3 * c_b

    tabs_p = _rope_tables(jnp.arange(tp), d_map, rope_dim)
    tabs_s = tuple(jnp.tile(t, (bs, 1)) for t in _rope_tables(past_len + jnp.arange(ts), d_map, rope_dim))
    in_g, in_b = ln_in_g.reshape(1, d_model), ln_in_b.reshape(1, d_model)

    xp = x_prompt.reshape(bp * tp, d_model)
    xs = x_sample.reshape(bs * ts, d_model)
    outs = [[] for _ in range(8)]
    for l in range(depth):
        lam_init = 0.8 - 0.6 * math.exp(-0.3 * l)
        first = l == 0
        w = w_in[l]
        o_ba = c_qkv + c_z
        o_b = o_ba + 2 * n_heads_a
        w_ba = jnp.pad(w[:, o_ba:o_b], ((0, 0), (0, LANES - 2 * n_heads_a)))
        w_in_f32 = jnp.concatenate([w[:, :o_ba], w[:, o_b:], w_ba], axis=1)
        w_in_l = w_in_f32.astype(BF16)
        w_out_l = w_out[l].astype(BF16)
        ne = n_groups * n_exp
        w_r = jnp.pad(jnp.concatenate([w_router_expert[l], w_router_group[l]], axis=1),
                      ((0, 0), (0, LANES - ne - n_groups)))
        b_r = _pad_lanes(jnp.concatenate([b_router_expert[l], b_router_group[l]]))
        wg = w_gate[l].reshape(ne, d_model, d_expert).astype(BF16)
        wu = w_up[l].reshape(ne, d_model, d_expert).astype(BF16)
        wd = w_down[l].reshape(ne, d_expert, d_model).astype(BF16)
        alog_pad = _pad_lanes(a_log[l], n_heads_a)
        dt_pad = _pad_lanes(dt_bias[l], n_heads_a)
        gdn_w = gdn_norm_w[l].reshape(1, dv)
        diff_w = diff_norm_w[l].reshape(1, dh)
        lamv = jnp.concatenate([_pad_lanes(v) for v in (lambda_q1[l], lambda_k1[l], lambda_q2[l], lambda_k2[l])]
                               + [jnp.zeros((SUBLANES - 4, LANES), F32)], axis=0)
        ln1 = (ln1_g[l].reshape(1, d_model), ln1_b[l].reshape(1, d_model))
        ln2 = (ln2_g[l].reshape(1, d_model), ln2_b[l].reshape(1, d_model))
        proj = functools.partial(_in_proj, apply_ln=first, c_qkv=c_qkv, c_z=c_z, c_b=c_b, q_scale=q_scale,
                                 rope_half=rope_dim // 2)
        mix_out = functools.partial(_out_proj, apply_ln=first, alpha=alpha, n_groups=n_groups, n_exp=n_exp)
        heads_a = dict(n_heads=n_heads_a, dk=dk, dv=dv)
        heads_b = dict(n_heads=n_heads_b, d_map=d_map, lam_init=lam_init)

        qkv, z, ba, q16, k32, k16, v32, v16 = proj(xp, in_g, in_b, w_in_l, tabs_p)
        shape3 = lambda a: a.reshape(bp, tp, a.shape[1])
        o_a, s_new = _gdn_prompt(shape3(qkv), shape3(z), shape3(ba), conv_w[l], alog_pad, dt_pad, gdn_w, **heads_a)
        o_bp = _attn_prompt(shape3(q16), shape3(k16), shape3(v16), lamv, diff_w, **heads_b)
        x1, x1b, gates = mix_out(o_a.reshape(bp * tp, c_z), o_bp.reshape(bp * tp, c_b), xp, in_g, in_b,
                                 w_out_l, *ln1, w_r, b_r)
        conv_rows = shape3(qkv)[:, tp - (kw - 1):, :]
        if tp < kw - 1:
            conv_rows = jnp.pad(shape3(qkv), ((0, 0), (kw - 1 - tp, 0), (0, 0)))
        xp = _moe(x1b, x1, gates, wg, wu, wd, *ln2, alpha=alpha, n_groups=n_groups)
        for i, a in enumerate((k32.reshape(bp, tp, n_heads_b, dh), v32.reshape(bp, tp, n_heads_b, dh),
                               s_new, conv_rows)):
            outs[i].append(a)

        qkv, z, ba, q16, k32, _, v32, _ = proj(xs, in_g, in_b, w_in_f32, tabs_s)
        o_a, s_new, c_new = _gdn_step(qkv, state_conv[l].reshape(bs, (kw - 1) * c_qkv), z, ba, conv_w[l],
                                      alog_pad, dt_pad, gdn_w, state_delta[l], **heads_a)
        o_bs = _attn_decode(q16, k32, v32, cache_k, cache_v, page_table, l, lamv, diff_w, **heads_b)
        x1, x1b, gates = mix_out(o_a, o_bs, xs, in_g, in_b, w_out[l], *ln1, w_r, b_r)
        xs = _moe(x1b, x1, gates, wg, wu, wd, *ln2, alpha=alpha, n_groups=n_groups)
        for i, a in enumerate((k32.reshape(bs, ts, n_heads_b, dh), v32.reshape(bs, ts, n_heads_b, dh),
                               s_new, c_new.reshape(bs, kw - 1, c_qkv))):
            outs[4 + i].append(a)

    return (xp.reshape(bp, tp, d_model), xs.reshape(bs, ts, d_model)) + tuple(jnp.stack(o) for o in outs)
```

```python
import functools
import math

import jax
import jax.numpy as jnp
from jax import lax
from jax.experimental import pallas as pl
from jax.experimental.pallas import tpu as pltpu

F32 = jnp.float32
BF16 = jnp.bfloat16
HIGHEST = lax.Precision.HIGHEST

LANES = 128
SUBLANES = 8
VMEM_LIMIT_BYTES = 56 * 1024 * 1024

LN_EPS = 1e-5
RMS_EPS = 1e-6
ROPE_THETA = 500000.0
CONV_TAIL = SUBLANES
GDN_CHUNK = 64
GDN_GROUP = 8
DECODE_PAGES = 8
DECODE_SLOTS = 4
ATTN_GROUP = 4
NEG = -1e30


def _pick_tile(n, preferred):
    if n <= preferred:
        return n
    for t in range(preferred, 0, -1):
        if n % t == 0 and t % SUBLANES == 0:
            return t
    return n


def _params(*semantics):
    return pltpu.CompilerParams(dimension_semantics=semantics, vmem_limit_bytes=VMEM_LIMIT_BYTES)


def _dot(a, b, precision=None):
    return jnp.dot(a, b, preferred_element_type=F32, precision=precision)


def _dot_nt(a, b, precision=None):
    return lax.dot_general(a, b, (((1,), (1,)), ((), ())), preferred_element_type=F32, precision=precision)


def _dot_tn(a, b, precision=None):
    return lax.dot_general(a, b, (((0,), (0,)), ((), ())), preferred_element_type=F32, precision=precision)


def _layer_norm(x, g, b):
    mu = jnp.mean(x, axis=-1, keepdims=True)
    xc = x - mu
    var = jnp.mean(xc * xc, axis=-1, keepdims=True)
    return xc * lax.rsqrt(var + LN_EPS) * g + b


def _rms_norm(x, w):
    return x * lax.rsqrt(jnp.mean(x * x, axis=-1, keepdims=True) + RMS_EPS) * w


def _l2_normalize(x):
    return x * lax.rsqrt(jnp.sum(x * x, axis=-1, keepdims=True) + RMS_EPS)


def _silu(x):
    return x * jax.nn.sigmoid(x)


def _softplus(x):
    return jnp.maximum(x, 0.0) + jnp.log1p(jnp.exp(-jnp.abs(x)))


def _lambda(lamv, lam_init):
    s1 = jnp.sum(lamv[0:1] * lamv[1:2], axis=-1, keepdims=True)
    s2 = jnp.sum(lamv[2:3] * lamv[3:4], axis=-1, keepdims=True)
    return jnp.exp(s1) - jnp.exp(s2) + lam_init


def _in_proj_kernel(x_ref, g_ref, b_ref, w_ref, cos_ref, sa_ref, sb_ref,
                    qkv_ref, z_ref, ba_ref, q16_ref, k32_ref, k16_ref, v32_ref, v16_ref,
                    *, apply_ln, c_qkv, c_z, c_b, q_scale, rope_half):
    x = x_ref[...]
    if apply_ln:
        x = _layer_norm(x, g_ref[...], b_ref[...])
    precise = w_ref.dtype == F32
    xin = x if precise else x.astype(BF16)

    def proj(lo, width):
        return _dot(xin, w_ref[:, lo:lo + width], HIGHEST if precise else None)

    qkv_ref[...] = proj(0, c_qkv)
    z_ref[...] = proj(c_qkv, c_z)
    base = c_qkv + c_z
    q = proj(base, c_b)
    k = proj(base + c_b, c_b)
    v = proj(base + 2 * c_b, c_b)
    ba_ref[...] = proj(base + 3 * c_b, LANES)

    cos = cos_ref[...]
    sa = sa_ref[...]
    sb = sb_ref[...]

    def rope(y):
        outs = []
        for h in range(c_b // LANES):
            yh = y[:, h * LANES:(h + 1) * LANES]
            up = pltpu.roll(yh, LANES - rope_half, 1)
            dn = pltpu.roll(yh, rope_half, 1)
            outs.append(yh * cos + up * sa + dn * sb)
        return jnp.concatenate(outs, axis=-1)

    q = rope(q) * q_scale
    k = rope(k)
    q16_ref[...] = q.astype(q16_ref.dtype)
    k16_ref[...] = k.astype(BF16)
    v16_ref[...] = v.astype(BF16)
    tm = x_ref.shape[0]
    n_b = c_b // LANES
    for h in range(n_b):
        k32_ref[pl.ds(h, tm, stride=n_b), :] = k[:, h * LANES:(h + 1) * LANES]
        v32_ref[pl.ds(h, tm, stride=n_b), :] = v[:, h * LANES:(h + 1) * LANES]


def _in_proj(x2d, ln_g, ln_b, w_bf16, tabs, *, apply_ln, c_qkv, c_z, c_b, q_scale, rope_half):
    n, d = x2d.shape
    cos, sa, sb = tabs
    tm = _pick_tile(math.gcd(n, cos.shape[0]), 512)
    tab_tiles = cos.shape[0] // tm
    row = lambda i: (i, 0)
    fixed = lambda i: (0, 0)
    tab = lambda i: (i % tab_tiles, 0)
    n_b = c_b // LANES
    outs = [
        jax.ShapeDtypeStruct((n, c_qkv), F32), jax.ShapeDtypeStruct((n, c_z), F32),
        jax.ShapeDtypeStruct((n, LANES), F32), jax.ShapeDtypeStruct((n, c_b), w_bf16.dtype),
        jax.ShapeDtypeStruct((n * n_b, LANES), F32), jax.ShapeDtypeStruct((n, c_b), BF16),
        jax.ShapeDtypeStruct((n * n_b, LANES), F32), jax.ShapeDtypeStruct((n, c_b), BF16),
    ]
    return pl.pallas_call(
        functools.partial(_in_proj_kernel, apply_ln=apply_ln, c_qkv=c_qkv, c_z=c_z, c_b=c_b,
                          q_scale=q_scale, rope_half=rope_half),
        out_shape=outs,
        grid=(n // tm,),
        in_specs=[
            pl.BlockSpec((tm, d), row), pl.BlockSpec((1, d), fixed), pl.BlockSpec((1, d), fixed),
            pl.BlockSpec(w_bf16.shape, fixed),
            pl.BlockSpec((tm, LANES), tab), pl.BlockSpec((tm, LANES), tab), pl.BlockSpec((tm, LANES), tab),
        ],
        out_specs=[pl.BlockSpec((tm * o.shape[0] // n, o.shape[1]), row) for o in outs],
        compiler_params=_params("parallel"),
        name="in_proj",
    )(x2d, ln_g, ln_b, w_bf16, cos, sa, sb)


def _rope_tables(pos, d_map, rope_dim):
    half = rope_dim // 2
    inv = ROPE_THETA ** (-jnp.arange(0, rope_dim, 2, dtype=F32) / rope_dim)
    ang = pos.astype(F32)[:, None] * inv
    cos, sin = jnp.cos(ang), jnp.sin(ang)
    n = pos.shape[0]
    ones = jnp.ones((n, d_map - rope_dim), F32)
    zeros_rest = jnp.zeros((n, d_map - rope_dim), F32)
    zeros_half = jnp.zeros((n, half), F32)
    cos_m = jnp.concatenate([cos, cos, ones], axis=1)
    sa_m = jnp.concatenate([-sin, zeros_half, zeros_rest], axis=1)
    sb_m = jnp.concatenate([zeros_half, sin, zeros_rest], axis=1)
    return tuple(jnp.concatenate([t, t], axis=1) for t in (cos_m, sa_m, sb_m))


def _split3(x):
    x1 = x.astype(BF16)
    r1 = x - x1.astype(F32)
    x2 = r1.astype(BF16)
    x3 = (r1 - x2.astype(F32)).astype(BF16)
    return x1, x2, x3


def _neumann_series(ms):
    n = ms[0].shape[0]
    accs = list(ms)
    qs = [_dot(m.astype(BF16), m.astype(BF16)) for m in ms]
    span = 4
    while span < n:
        rs = [_dot(jnp.concatenate([a.astype(BF16), q.astype(BF16)], axis=0), q.astype(BF16))
              for a, q in zip(accs, qs)]
        accs = [a + q + r[:n] for a, q, r in zip(accs, qs, rs)]
        qs = [r[n:] for r in rs]
        span *= 2
    return [a + q + _dot(a.astype(BF16), q.astype(BF16)) for a, q in zip(accs, qs)]


def _gdn_prompt_kernel(qkv_ref, z_ref, ba_ref, cw_ref, alog_ref, dt_ref, nw_ref,
                       o_ref, s_ref, y_scr, gam_scr, beta_scr, prev_scr, state_scr,
                       *, n_heads, dk, dv, chunk, group):
    t = pl.program_id(1)
    tt, c = qkv_ref.shape

    @pl.when(t == 0)
    def _():
        prev_scr[...] = jnp.zeros_like(prev_scr)
        state_scr[...] = jnp.zeros_like(state_scr)

    x = qkv_ref[...]
    prev = prev_scr[...]
    cw = cw_ref[...]
    kw = cw.shape[0]
    tail_row = lax.broadcasted_iota(jnp.int32, (CONV_TAIL, c), 0)
    acc = x * cw[kw - 1:kw, :]
    for j in range(1, kw):
        xs = pltpu.roll(x, j, 0)
        ps = pltpu.roll(prev, j, 0)
        head = jnp.where(tail_row < j, ps, xs[0:CONV_TAIL])
        xs = jnp.concatenate([head, xs[CONV_TAIL:]], axis=0)
        acc = acc + xs * cw[kw - 1 - j:kw - j, :]
    prev_scr[...] = x[tt - CONV_TAIL:tt]
    y_scr[...] = _silu(acc)

    ba = ba_ref[...]
    beta_scr[...] = jax.nn.sigmoid(ba)
    gam = -jnp.exp(alog_ref[...]) * _softplus(ba + dt_ref[...])
    row_in_chunk = lax.broadcasted_iota(jnp.int32, (tt, LANES), 0) % chunk
    step = 1
    while step < chunk:
        gam = gam + jnp.where(row_in_chunk >= step, pltpu.roll(gam, step, 0), 0.0)
        step *= 2
    gam_scr[...] = gam

    ri = lax.broadcasted_iota(jnp.int32, (chunk, chunk), 0)
    ci = lax.broadcasted_iota(jnp.int32, (chunk, chunk), 1)
    incl = ri >= ci
    strict = ri > ci
    sel_r = lax.broadcasted_iota(jnp.int32, (SUBLANES, LANES), 0)
    sel_c = lax.broadcasted_iota(jnp.int32, (SUBLANES, LANES), 1)
    head_rows = (sel_c == sel_r + n_heads).astype(BF16)
    nw = nw_ref[...]

    def chunk_group(ig, carry):
        starts = [pl.multiple_of((ig * group + c) * chunk, chunk) for c in range(group)]
        units = []
        for r0 in starts:
            gam = gam_scr[pl.ds(r0, chunk), :]
            b_c = beta_scr[pl.ds(r0, chunk), :]
            y_c = y_scr[pl.ds(r0, chunk), :]
            picked = _dot_nt(head_rows, jnp.concatenate(_split3(gam), axis=0))
            gam_rows = picked[:, :chunk] + picked[:, chunk:2 * chunk] + picked[:, 2 * chunk:]
            e_gam = jnp.exp(gam)
            g_last = gam[chunk - 1:chunk, :]
            k_end_scale = jnp.exp(g_last - gam)
            g_end = jnp.exp(g_last)
            for h in range(n_heads):
                gl = n_heads + h
                qh = _l2_normalize(y_c[:, h * dk:(h + 1) * dk]) * (dk ** -0.5)
                kh = _l2_normalize(y_c[:, (n_heads + h) * dk:(n_heads + h + 1) * dk])
                vh = y_c[:, 2 * n_heads * dk + h * dv:2 * n_heads * dk + (h + 1) * dv]
                diff = gam[:, gl:gl + 1] - gam_rows[h:h + 1, :]
                units.append(dict(
                    qh=qh, kh=kh, vh=vh, beta=b_c[:, h:h + 1], eg=e_gam[:, gl:gl + 1],
                    decay=jnp.where(incl, jnp.exp(jnp.where(incl, diff, 0.0)), 0.0),
                    k_end=k_end_scale[:, gl:gl + 1] * kh, g_end=g_end[:, gl:gl + 1]))
        qk_kk = [_dot_nt(jnp.concatenate([u["qh"], u["kh"]], axis=0).astype(BF16), u["kh"].astype(BF16))
                 for u in units]
        attn = [x[:chunk] * u["decay"] for x, u in zip(qk_kk, units)]
        ms = [jnp.where(strict, -(u["beta"] * x[chunk:] * u["decay"]), 0.0) for x, u in zip(qk_kk, units)]
        inv_minus_i = _neumann_series(ms)
        rhs = [jnp.concatenate([u["beta"] * u["vh"], (u["beta"] * u["eg"]) * u["kh"]], axis=-1) for u in units]
        sols = [r + _dot(n.astype(BF16), r.astype(BF16)) for n, r in zip(inv_minus_i, rhs)]

        states = [state_scr[h] for h in range(n_heads)]
        for c, r0 in enumerate(starts):
            ids = range(c * n_heads, (c + 1) * n_heads)
            wq = [_dot(jnp.concatenate([sols[i][:, dv:], units[i]["eg"] * units[i]["qh"]], axis=0).astype(BF16),
                       s.astype(BF16)) for i, s in zip(ids, states)]
            ub = [(sols[i][:, :dv] - x[:chunk]).astype(BF16) for i, x in zip(ids, wq)]
            o = [x[chunk:] + _dot(attn[i].astype(BF16), u) for i, x, u in zip(ids, wq, ub)]
            states = [units[i]["g_end"] * s + _dot_tn(units[i]["k_end"].astype(BF16), u)
                      for i, s, u in zip(ids, states, ub)]
            z_c = z_ref[pl.ds(r0, chunk), :]
            o_ref[pl.ds(r0, chunk), :] = jnp.concatenate(
                [_rms_norm(x, nw) * _silu(z_c[:, h * dv:(h + 1) * dv]) for h, x in enumerate(o)], axis=-1)
        for h in range(n_heads):
            state_scr[h] = states[h]
        return carry

    lax.fori_loop(0, tt // (group * chunk), chunk_group, 0)

    @pl.when(t == pl.num_programs(1) - 1)
    def _():
        s_ref[...] = state_scr[...]


def _gdn_prompt(qkv, z, ba, conv_w, alog_pad, dt_pad, norm_w, *, n_heads, dk, dv):
    b, t, c = qkv.shape
    tt = _pick_tile(t, 512)
    group = math.gcd(tt // GDN_CHUNK, GDN_GROUP)
    assert tt % (group * GDN_CHUNK) == 0 and tt >= CONV_TAIL
    tile = lambda ib, it: (ib, it, 0)
    fixed = lambda ib, it: (0, 0)
    return pl.pallas_call(
        functools.partial(_gdn_prompt_kernel, n_heads=n_heads, dk=dk, dv=dv, chunk=GDN_CHUNK, group=group),
        out_shape=[jax.ShapeDtypeStruct((b, t, n_heads * dv), F32),
                   jax.ShapeDtypeStruct((b, n_heads, dk, dv), F32)],
        grid=(b, t // tt),
        in_specs=[
            pl.BlockSpec((None, tt, c), tile), pl.BlockSpec((None, tt, n_heads * dv), tile),
            pl.BlockSpec((None, tt, LANES), tile),
            pl.BlockSpec(conv_w.shape, fixed), pl.BlockSpec((1, LANES), fixed), pl.BlockSpec((1, LANES), fixed),
            pl.BlockSpec((1, dv), fixed),
        ],
        out_specs=[pl.BlockSpec((None, tt, n_heads * dv), tile),
                   pl.BlockSpec((None, n_heads, dk, dv), lambda ib, it: (ib, 0, 0, 0))],
        scratch_shapes=[pltpu.VMEM((tt, c), F32), pltpu.VMEM((tt, LANES), F32), pltpu.VMEM((tt, LANES), F32),
                        pltpu.VMEM((CONV_TAIL, c), F32), pltpu.VMEM((n_heads, dk, dv), F32)],
        compiler_params=_params("parallel", "arbitrary"),
        name="gdn_prompt",
    )(qkv, z, ba, conv_w, alog_pad, dt_pad, norm_w)


def _gdn_step_kernel(qkv_ref, sc_ref, z_ref, ba_ref, cw_ref, alog_ref, dt_ref, nw_ref, s_ref,
                     o_ref, sn_ref, cn_ref, *, n_heads, dk, dv):
    nb, c = qkv_ref.shape
    x = qkv_ref[...]
    sc = sc_ref[...]
    cw = cw_ref[...]
    kw = cw.shape[0]
    acc = x * cw[kw - 1:kw, :]
    for i in range(kw - 1):
        acc = acc + sc[:, i * c:(i + 1) * c] * cw[i:i + 1, :]
    y = _silu(acc)
    cn_ref[...] = jnp.concatenate([sc[:, c:], x], axis=-1)

    ba = ba_ref[...]
    beta = jax.nn.sigmoid(ba)
    e_g = jnp.exp(-jnp.exp(alog_ref[...]) * _softplus(ba + dt_ref[...]))
    nw = nw_ref[...]
    pad_k = jnp.zeros((SUBLANES - 2, dk), F32)
    pad_1k = jnp.zeros((SUBLANES - 1, dk), F32)
    pad_1v = jnp.zeros((SUBLANES - 1, dv), F32)
    outs = []
    for h in range(n_heads):
        gl = n_heads + h
        qh = _l2_normalize(y[:, h * dk:(h + 1) * dk]) * (dk ** -0.5)
        kh = _l2_normalize(y[:, (n_heads + h) * dk:(n_heads + h + 1) * dk])
        vh = y[:, 2 * n_heads * dk + h * dv:2 * n_heads * dk + (h + 1) * dv]
        beta_h = beta[:, h:h + 1]
        eg_h = e_g[:, gl:gl + 1]
        qk = jnp.sum(qh * kh, axis=-1, keepdims=True)
        rows = []
        for ib in range(nb):
            s = s_ref[ib, h]
            k_row = kh[ib:ib + 1]
            lhs = jnp.concatenate([k_row, qh[ib:ib + 1], pad_k], axis=0)
            kq = _dot(lhs, s, HIGHEST)
            b_i = beta_h[ib:ib + 1]
            e_i = eg_h[ib:ib + 1]
            u = b_i * vh[ib:ib + 1] - (b_i * e_i) * kq[0:1]
            rows.append(e_i * kq[1:2] + qk[ib:ib + 1] * u)
            outer = _dot_tn(jnp.concatenate([k_row, pad_1k], axis=0), jnp.concatenate([u, pad_1v], axis=0),
                            HIGHEST)
            sn_ref[ib, h] = e_i * s + outer
        o = jnp.concatenate(rows, axis=0)
        outs.append(_rms_norm(o, nw) * _silu(z_ref[:, h * dv:(h + 1) * dv]))
    o_ref[...] = jnp.concatenate(outs, axis=-1)


def _gdn_step(qkv, conv_state, z, ba, conv_w, alog_pad, dt_pad, norm_w, state, *, n_heads, dk, dv):
    n, c = qkv.shape
    nb = _pick_tile(n, SUBLANES)
    row = lambda i: (i, 0)
    fixed = lambda i: (0, 0)
    st = lambda i: (i, 0, 0, 0)
    return pl.pallas_call(
        functools.partial(_gdn_step_kernel, n_heads=n_heads, dk=dk, dv=dv),
        out_shape=[jax.ShapeDtypeStruct((n, n_heads * dv), F32), jax.ShapeDtypeStruct(state.shape, F32),
                   jax.ShapeDtypeStruct(conv_state.shape, F32)],
        grid=(n // nb,),
        in_specs=[
            pl.BlockSpec((nb, c), row), pl.BlockSpec((nb, conv_state.shape[1]), row),
            pl.BlockSpec((nb, n_heads * dv), row), pl.BlockSpec((nb, LANES), row),
            pl.BlockSpec(conv_w.shape, fixed), pl.BlockSpec((1, LANES), fixed), pl.BlockSpec((1, LANES), fixed),
            pl.BlockSpec((1, dv), fixed), pl.BlockSpec((nb, n_heads, dk, dv), st),
        ],
        out_specs=[pl.BlockSpec((nb, n_heads * dv), row), pl.BlockSpec((nb, n_heads, dk, dv), st),
                   pl.BlockSpec((nb, conv_state.shape[1]), row)],
        compiler_params=_params("parallel"),
        name="gdn_step",
    )(qkv, conv_state, z, ba, conv_w, alog_pad, dt_pad, norm_w, state)


def _softmax_step(s, v_bf16, m, l, acc):
    m_new = jnp.maximum(m, jnp.max(s, axis=-1, keepdims=True))
    alpha = jnp.exp(m - m_new)
    p = jnp.exp(s - m_new)
    l = alpha * l + jnp.sum(p, axis=-1, keepdims=True)
    acc = alpha * acc + _dot(p.astype(BF16), v_bf16)
    return m_new, l, acc


def _attn_prompt_kernel(lamv_ref, nw_ref, q_ref, k_ref, v_ref, o_ref, *, tk, d_map, lam_init):
    iq = pl.program_id(2)
    tq, dh = q_ref.shape
    q = q_ref[...]
    lane = lax.broadcasted_iota(jnp.int32, (tq, dh), 1)
    zero = jnp.zeros_like(q)
    q2 = jnp.concatenate([jnp.where(lane < d_map, q, zero), jnp.where(lane >= d_map, q, zero)], axis=0)
    rows = 2 * tq
    n_full = (iq * tq) // tk

    def scores(j):
        return _dot_nt(q2, k_ref[pl.ds(pl.multiple_of(j * tk, tk), tk), :])

    def values(j):
        return v_ref[pl.ds(pl.multiple_of(j * tk, tk), tk), :]

    def block_group(size, first):
        def body(jg, carry):
            j0 = first + size * jg
            s = [scores(j0 + i) for i in range(size)]
            for i in range(size):
                carry = _softmax_step(s[i], values(j0 + i), *carry)
            return carry
        return body

    carry = (jnp.full((rows, 1), NEG, F32), jnp.zeros((rows, 1), F32), jnp.zeros((rows, dh), F32))
    first = 0
    for size in (ATTN_GROUP, 1):
        trips = (n_full - first) // size
        carry = lax.fori_loop(0, trips, block_group(size, first), carry)
        first = first + trips * size

    r = lax.broadcasted_iota(jnp.int32, (rows, tk), 0)
    col_minus_row = lax.broadcasted_iota(jnp.int32, (rows, tk), 1) - jnp.where(r < tq, r, r - tq)
    s = jnp.where(col_minus_row <= iq * tq - n_full * tk, scores(n_full), NEG)
    _, l, acc = _softmax_step(s, values(n_full), *carry)
    on = acc / l
    lam = _lambda(lamv_ref[...], lam_init)
    o = on[:tq] - lam * on[tq:]
    o_ref[...] = _rms_norm(o, nw_ref[...]) * (1.0 - lam_init)


def _attn_prompt(q16, k16, v16, lamv, norm_w, *, n_heads, d_map, lam_init):
    b, t, _ = q16.shape
    dh = 2 * d_map
    tq = _pick_tile(t, 512)
    tk = _pick_tile(t, 512)
    assert tk % tq == 0, "one key block must cover a query tile's diagonal"
    fixed = lambda ib, ih, iq: (0, 0)
    return pl.pallas_call(
        functools.partial(_attn_prompt_kernel, tk=tk, d_map=d_map, lam_init=lam_init),
        out_shape=jax.ShapeDtypeStruct((b, t, n_heads * dh), F32),
        grid=(b, n_heads, t // tq),
        in_specs=[
            pl.BlockSpec(lamv.shape, fixed), pl.BlockSpec((1, dh), fixed),
            pl.BlockSpec((None, tq, dh), lambda ib, ih, iq: (ib, iq, ih)),
            pl.BlockSpec((None, t, dh), lambda ib, ih, iq: (ib, 0, ih)),
            pl.BlockSpec((None, t, dh), lambda ib, ih, iq: (ib, 0, ih)),
        ],
        out_specs=pl.BlockSpec((None, tq, dh), lambda ib, ih, iq: (ib, iq, ih)),
        compiler_params=_params("parallel", "parallel", "arbitrary"),
        name="attn_prompt",
    )(lamv, norm_w, q16, k16, v16)


def _attn_decode_kernel(pt_ref, lamv_ref, nw_ref, q_ref, kn_ref, vn_ref, ck_ref, cv_ref, o_ref,
                        kbuf, vbuf, sems, *, pages, page_base, n_heads, d_map, lam_init):
    ib = pl.program_id(0)
    n_seq = pl.num_programs(0)
    n_chunks = pt_ref.shape[1] // pages
    dh = 2 * d_map
    page_rows = kbuf.shape[1] // pages
    rows = 2 * SUBLANES

    def page_copies(seq, chunk, slot):
        copies = []
        for p in range(pages):
            page = page_base + pt_ref[seq, chunk * pages + p]
            dst = pl.ds(p * page_rows, page_rows)
            copies.append(pltpu.make_async_copy(ck_ref.at[page], kbuf.at[slot, dst, :], sems.at[slot, 0]))
            copies.append(pltpu.make_async_copy(cv_ref.at[page], vbuf.at[slot, dst, :], sems.at[slot, 1]))
        return copies

    def head_rows(x, mask_maps):
        lane = lax.broadcasted_iota(jnp.int32, (1, dh), 1)
        out = []
        for m in range(2):
            for h in range(n_heads):
                xh = x[:, h * dh:(h + 1) * dh]
                if mask_maps:
                    xh = jnp.where((lane >= m * d_map) & (lane < (m + 1) * d_map), xh, 0.0)
                out.append(xh)
        out.append(jnp.zeros((rows - 2 * n_heads, dh), F32))
        return jnp.concatenate(out, axis=0)

    def two_term_dot(dot, a, b):
        a_hi = a.astype(BF16)
        a_lo = (a - a_hi.astype(F32)).astype(BF16)
        b_hi = b.astype(BF16)
        b_lo = (b - b_hi.astype(F32)).astype(BF16)
        both = dot(jnp.concatenate([a_hi, a_lo], axis=0), b_hi)
        return both[:rows] + both[rows:] + dot(a_hi, b_lo)

    n_slots = kbuf.shape[0]
    ahead = n_slots - 1

    def start_chunk(g_seq, g_chunk, slot):
        seq = g_seq + g_chunk // n_chunks

        @pl.when(seq < n_seq)
        def _():
            for cp in page_copies(seq, g_chunk % n_chunks, slot):
                cp.start()

    @pl.when(ib == 0)
    def _():
        for g in range(ahead):
            start_chunk(0, g, g % n_slots)

    qmat = head_rows(q_ref[...].astype(F32), True)
    n_cols = pages * page_rows
    own_head = (lax.broadcasted_iota(jnp.int32, (rows, n_cols), 1) % n_heads
                == lax.broadcasted_iota(jnp.int32, (rows, n_cols), 0) % n_heads)

    def chunk_body(ic, carry):
        m_old, l_old, acc_old = carry
        g = ib * n_chunks + ic
        slot = g % n_slots
        start_chunk(ib, ic + ahead, (g + ahead) % n_slots)
        for cp in page_copies(ib, ic, slot):
            cp.wait()
        s = jnp.where(own_head, two_term_dot(_dot_nt, qmat, kbuf[slot]), NEG)
        m_new = jnp.maximum(m_old, jnp.max(s, axis=-1, keepdims=True))
        alpha = jnp.exp(m_old - m_new)
        p = jnp.exp(s - m_new)
        l_new = alpha * l_old + jnp.sum(p, axis=-1, keepdims=True)
        acc_new = alpha * acc_old + two_term_dot(_dot, p, vbuf[slot])
        return m_new, l_new, acc_new

    init = (jnp.full((rows, 1), NEG, F32), jnp.zeros((rows, 1), F32), jnp.zeros((rows, dh), F32))
    m_c, l_c, acc_c = lax.fori_loop(0, n_chunks, chunk_body, init)

    s_new = jnp.sum(qmat * head_rows(kn_ref[...], False), axis=-1, keepdims=True)
    m_f = jnp.maximum(m_c, s_new)
    a = jnp.exp(m_c - m_f)
    p_new = jnp.exp(s_new - m_f)
    l_f = a * l_c + p_new
    acc_f = a * acc_c + p_new * head_rows(vn_ref[...], False)
    on = acc_f / l_f
    lam = _lambda(lamv_ref[...], lam_init)
    o = on[0:n_heads] - lam * on[n_heads:2 * n_heads]
    o = _rms_norm(o, nw_ref[...]) * (1.0 - lam_init)
    o_ref[...] = jnp.concatenate([o[h:h + 1] for h in range(n_heads)], axis=-1)


def _attn_decode(q, k_new, v_new, cache_k, cache_v, page_table, layer, lamv, norm_w,
                 *, n_heads, d_map, lam_init):
    bs = q.shape[0]
    dh = 2 * d_map
    depth, n_pool, page, _, _ = cache_k.shape
    n_pages = page_table.shape[1]
    pages = math.gcd(n_pages, DECODE_PAGES)
    page_rows = page * n_heads
    assert 2 * SUBLANES >= 2 * n_heads
    ck = cache_k.reshape(depth * n_pool, page_rows, dh)
    cv = cache_v.reshape(depth * n_pool, page_rows, dh)
    q3 = q.reshape(bs, 1, n_heads * dh)
    kn3 = k_new.reshape(bs, 1, n_heads * dh)
    vn3 = v_new.reshape(bs, 1, n_heads * dh)
    fixed = lambda ib, pt: (0, 0)
    seq = lambda ib, pt: (ib, 0, 0)
    grid_spec = pltpu.PrefetchScalarGridSpec(
        num_scalar_prefetch=1,
        grid=(bs,),
        in_specs=[pl.BlockSpec(lamv.shape, fixed), pl.BlockSpec((1, dh), fixed),
                  pl.BlockSpec((None, 1, n_heads * dh), seq), pl.BlockSpec((None, 1, n_heads * dh), seq),
                  pl.BlockSpec((None, 1, n_heads * dh), seq),
                  pl.BlockSpec(memory_space=pl.ANY), pl.BlockSpec(memory_space=pl.ANY)],
        out_specs=pl.BlockSpec((None, 1, n_heads * dh), seq),
        scratch_shapes=[pltpu.VMEM((DECODE_SLOTS, pages * page_rows, dh), F32),
                        pltpu.VMEM((DECODE_SLOTS, pages * page_rows, dh), F32),
                        pltpu.SemaphoreType.DMA((DECODE_SLOTS, 2))],
    )
    out = pl.pallas_call(
        functools.partial(_attn_decode_kernel, pages=pages, page_base=layer * n_pool, n_heads=n_heads,
                          d_map=d_map, lam_init=lam_init),
        out_shape=jax.ShapeDtypeStruct((bs, 1, n_heads * dh), F32),
        grid_spec=grid_spec,
        compiler_params=_params("arbitrary"),
        name="attn_decode",
    )(page_table, lamv, norm_w, q3, kn3, vn3, ck, cv)
    return out.reshape(bs, n_heads * dh)


def _route(logits, n_groups, n_exp):
    lane = lax.broadcasted_iota(jnp.int32, logits.shape, 1)
    ne = n_groups * n_exp
    big = jnp.int32(LANES)

    def first_max(mask):
        val = jnp.max(jnp.where(mask, logits, NEG), axis=-1, keepdims=True)
        idx = jnp.min(jnp.where(mask & (logits == val), lane, big), axis=-1, keepdims=True)
        return val, idx

    is_group = (lane >= ne) & (lane < ne + n_groups)
    g_max, g_lane = first_max(is_group)
    g_sum = jnp.sum(jnp.where(is_group, jnp.exp(jnp.where(is_group, logits, g_max) - g_max), 0.0),
                    axis=-1, keepdims=True)
    g_w = 1.0 / g_sum
    lo = (g_lane - ne) * n_exp
    in_group = (lane >= lo) & (lane < lo + n_exp)
    v1, i1 = first_max(in_group)
    v2, i2 = first_max(in_group & (lane != i1))
    e = jnp.exp(v2 - v1)
    w1 = g_w / (1.0 + e)
    w2 = g_w * e / (1.0 + e)
    gates = jnp.where(lane == i1, w1, jnp.where(lane == i2, w2, 0.0))
    return jnp.where(lane == ne, (g_lane - ne).astype(F32), gates)


def _out_proj_kernel(oa_ref, ob_ref, x_ref, ing_ref, inb_ref, w_ref, g_ref, b_ref, wr_ref, br_ref,
                     x1_ref, x1b_ref, gate_ref, *, apply_ln, alpha, n_groups, n_exp):
    x = x_ref[...]
    if apply_ln:
        x = _layer_norm(x, ing_ref[...], inb_ref[...])
    wa = oa_ref.shape[1]
    precise = w_ref.dtype == F32
    mix_t = F32 if precise else BF16
    prec = HIGHEST if precise else None
    y = (_dot(oa_ref[...].astype(mix_t), w_ref[:wa, :], prec)
         + _dot(ob_ref[...].astype(mix_t), w_ref[wa:, :], prec))
    x1 = _layer_norm(alpha * x + y, g_ref[...], b_ref[...])
    x1_ref[...] = x1
    x_hi = x1.astype(BF16)
    x1b_ref[...] = x_hi
    x_lo = (x1 - x_hi.astype(F32)).astype(BF16)
    w_r = wr_ref[...]
    w_hi = w_r.astype(BF16)
    w_lo = (w_r - w_hi.astype(F32)).astype(BF16)
    logits = _dot(x_hi, w_hi) + _dot(x_hi, w_lo) + _dot(x_lo, w_hi) + br_ref[...]
    gate_ref[...] = _route(logits, n_groups, n_exp)


def _out_proj(o_a, o_b, x2d, in_g, in_b, w_bf16, g, b, w_router, b_router, *, apply_ln, alpha, n_groups, n_exp):
    n, d = x2d.shape
    tm = _pick_tile(n, 512)
    row = lambda i: (i, 0)
    fixed = lambda i: (0, 0)
    return pl.pallas_call(
        functools.partial(_out_proj_kernel, apply_ln=apply_ln, alpha=alpha, n_groups=n_groups, n_exp=n_exp),
        out_shape=[jax.ShapeDtypeStruct((n, d), F32), jax.ShapeDtypeStruct((n, d), BF16),
                   jax.ShapeDtypeStruct((n, LANES), F32)],
        grid=(n // tm,),
        in_specs=[
            pl.BlockSpec((tm, o_a.shape[1]), row), pl.BlockSpec((tm, o_b.shape[1]), row), pl.BlockSpec((tm, d), row),
            pl.BlockSpec((1, d), fixed), pl.BlockSpec((1, d), fixed), pl.BlockSpec(w_bf16.shape, fixed),
            pl.BlockSpec((1, d), fixed), pl.BlockSpec((1, d), fixed),
            pl.BlockSpec(w_router.shape, fixed), pl.BlockSpec((1, LANES), fixed),
        ],
        out_specs=[pl.BlockSpec((tm, d), row), pl.BlockSpec((tm, d), row), pl.BlockSpec((tm, LANES), row)],
        compiler_params=_params("parallel"),
        name="out_proj",
    )(o_a, o_b, x2d, in_g, in_b, w_bf16, g, b, w_router, b_router)


def _moe_kernel(xb_ref, x1_ref, gate_ref, wg_ref, wu_ref, wd_ref, g_ref, b_ref, o_ref,
                perm_scr, xs_scr, gs_scr, ys_scr, seg_smem, *, alpha, n_groups, parts, row_block):
    j = pl.program_id(1)
    tm, d = x1_ref.shape
    n_exp_part = wg_ref.shape[0]
    lane = lax.broadcasted_iota(jnp.int32, (tm, LANES), 1)

    @pl.when(j == 0)
    def _():
        gates = gate_ref[...]
        gid = gates[:, n_groups * parts * n_exp_part:n_groups * parts * n_exp_part + 1]
        member = jnp.where((lane < n_groups) & (lane.astype(F32) == gid), 1.0, 0.0)
        cum = member
        row = lax.broadcasted_iota(jnp.int32, (tm, LANES), 0)
        step = 1
        while step < tm:
            cum = cum + jnp.where(row >= step, pltpu.roll(cum, step, 0), 0.0)
            step *= 2
        counts = cum[tm - 1:tm, :]
        offs = jnp.zeros_like(counts)
        for k in range(1, n_groups):
            offs = offs + pltpu.roll(counts, k, 1)
        pos = jnp.sum(member * (offs + cum - 1.0), axis=-1, keepdims=True).astype(jnp.int32)
        dest = lax.broadcasted_iota(jnp.int32, (tm, tm), 1)
        perm = jnp.where(dest == pos, 1.0, 0.0).astype(BF16)
        perm_scr[...] = perm
        xs_scr[...] = _dot_tn(perm, xb_ref[...]).astype(BF16)
        gs_scr[...] = sum(_dot_tn(perm, part) for part in _split3(gates))
        ys_scr[...] = jnp.zeros_like(ys_scr)
        lane1 = lax.broadcasted_iota(jnp.int32, (1, LANES), 1)
        for g in range(n_groups):
            seg_smem[0, g] = jnp.sum(jnp.where(lane1 == g, offs, 0.0)).astype(jnp.int32)
            seg_smem[1, g] = jnp.sum(jnp.where(lane1 == g, counts, 0.0)).astype(jnp.int32)

    off = seg_smem[0, j // parts]
    cnt = seg_smem[1, j // parts]
    rb_lane = lax.broadcasted_iota(jnp.int32, (row_block, LANES), 1)
    for lo in range(0, tm, row_block):
        @pl.when((cnt > 0) & (off < lo + row_block) & (off + cnt > lo))
        def _():
            xs = xs_scr[lo:lo + row_block, :]
            gs = gs_scr[lo:lo + row_block, :]
            hg = [_dot(xs, wg_ref[e]) for e in range(n_exp_part)]
            hu = [_dot(xs, wu_ref[e]) for e in range(n_exp_part)]
            acc = None
            for e in range(n_exp_part):
                gate_e = jnp.sum(jnp.where(rb_lane == j * n_exp_part + e, gs, 0.0), axis=-1, keepdims=True)
                y_e = _dot((_silu(hg[e]) * hu[e] * gate_e).astype(BF16), wd_ref[e])
                acc = y_e if acc is None else acc + y_e
            ys_scr[lo:lo + row_block, :] += acc

    @pl.when(j == pl.num_programs(1) - 1)
    def _():
        y = _dot(perm_scr[...], ys_scr[...].astype(BF16))
        o_ref[...] = _layer_norm(alpha * x1_ref[...] + y, g_ref[...], b_ref[...])


def _moe(x1b, x1, gates, wg, wu, wd, g, b, *, alpha, n_groups):
    n, d = x1.shape
    n_exp_total, _, f = wg.shape
    parts = 2
    n_exp_part = n_exp_total // (n_groups * parts)
    tm = _pick_tile(n, 1024)
    row_block = _pick_tile(tm, 256)
    row = lambda i, j: (i, 0)
    fixed = lambda i, j: (0, 0)
    part = lambda i, j: (j, 0, 0, 0)
    shape4 = lambda w: w.reshape(n_groups * parts, n_exp_part, w.shape[1], w.shape[2])
    return pl.pallas_call(
        functools.partial(_moe_kernel, alpha=alpha, n_groups=n_groups, parts=parts, row_block=row_block),
        out_shape=jax.ShapeDtypeStruct((n, d), F32),
        grid=(n // tm, n_groups * parts),
        in_specs=[
            pl.BlockSpec((tm, d), row), pl.BlockSpec((tm, d), row), pl.BlockSpec((tm, LANES), row),
            pl.BlockSpec((None, n_exp_part, d, f), part), pl.BlockSpec((None, n_exp_part, d, f), part),
            pl.BlockSpec((None, n_exp_part, f, d), part),
            pl.BlockSpec((1, d), fixed), pl.BlockSpec((1, d), fixed),
        ],
        out_specs=pl.BlockSpec((tm, d), row),
        scratch_shapes=[pltpu.VMEM((tm, tm), BF16), pltpu.VMEM((tm, d), BF16), pltpu.VMEM((tm, LANES), F32),
                        pltpu.VMEM((tm, d), F32), pltpu.SMEM((2, n_groups), jnp.int32)],
        compiler_params=_params("parallel", "arbitrary"),
        name="moe",
    )(x1b, x1, gates, shape4(wg), shape4(wu), shape4(wd), g, b)


def _pad_lanes(v, offset=0):
    return jnp.zeros((1, LANES), F32).at[0, offset:offset + v.shape[0]].set(v.astype(F32))


def kernel(x_prompt, x_sample, cache_k, cache_v, state_delta, state_conv, page_table, ln_in_g, ln_in_b, w_in, conv_w, a_log, dt_bias, gdn_norm_w, lambda_q1, lambda_k1, lambda_q2, lambda_k2, diff_norm_w, w_out, ln1_g, ln1_b, w_router_group, b_router_group, w_router_expert, b_router_expert, w_gate, w_up, w_down, ln2_g, ln2_b):
    bp, tp, d_model = x_prompt.shape
    bs, ts, _ = x_sample.shape
    depth = w_in.shape[0]
    n_heads_a = a_log.shape[1]
    dv = gdn_norm_w.shape[1]
    c_qkv = conv_w.shape[2]
    kw = conv_w.shape[1]
    c_z = n_heads_a * dv
    dk = (c_qkv - c_z) // (2 * n_heads_a)
    n_heads_b = cache_k.shape[3]
    d_map = lambda_q1.shape[1]
    dh = 2 * d_map
    c_b = n_heads_b * dh
    rope_dim = d_map // 4
    n_groups, n_exp = w_gate.shape[1], w_gate.shape[2]
    d_expert = w_gate.shape[4]
    n_pages = page_table.shape[1]
    past_len = n_pages * cache_k.shape[2]
    alpha = (2.0 * depth) ** 0.25
    q_scale = d_map ** -0.5

    assert ts == 1, "one new token per sample sequence"
    assert dh == LANES and dk == LANES and dv == LANES, "heads are one vreg wide"
    assert math.frexp(q_scale)[0] == 0.5, "score scale must be a power of two to fold into q exactly"
    assert 2 * n_heads_a <= LANES and n_groups * n_exp + n_groups <= LANES
    assert w_in.shape[2] == c_qkv + c_z + 2 * n_heads_a + 3 * c_b

    tabs_p = _rope_tables(jnp.arange(tp), d_map, rope_dim)
    tabs_s = tuple(jnp.tile(t, (bs, 1)) for t in _rope_tables(past_len + jnp.arange(ts), d_map, rope_dim))
    in_g, in_b = ln_in_g.reshape(1, d_model), ln_in_b.reshape(1, d_model)

    xp = x_prompt.reshape(bp * tp, d_model)
    xs = x_sample.reshape(bs * ts, d_model)
    outs = [[] for _ in range(8)]
    for l in range(depth):
        lam_init = 0.8 - 0.6 * math.exp(-0.3 * l)
        first = l == 0
        w = w_in[l]
        o_ba = c_qkv + c_z
        o_b = o_ba + 2 * n_heads_a
        w_ba = jnp.pad(w[:, o_ba:o_b], ((0, 0), (0, LANES - 2 * n_heads_a)))
        w_in_f32 = jnp.concatenate([w[:, :o_ba], w[:, o_b:], w_ba], axis=1)
        w_in_l = w_in_f32.astype(BF16)
        w_out_l = w_out[l].astype(BF16)
        ne = n_groups * n_exp
        w_r = jnp.pad(jnp.concatenate([w_router_expert[l], w_router_group[l]], axis=1),
                      ((0, 0), (0, LANES - ne - n_groups)))
        b_r = _pad_lanes(jnp.concatenate([b_router_expert[l], b_router_group[l]]))
        wg = w_gate[l].reshape(ne, d_model, d_expert).astype(BF16)
        wu = w_up[l].reshape(ne, d_model, d_expert).astype(BF16)
        wd = w_down[l].reshape(ne, d_expert, d_model).astype(BF16)
        alog_pad = _pad_lanes(a_log[l], n_heads_a)
        dt_pad = _pad_lanes(dt_bias[l], n_heads_a)
        gdn_w = gdn_norm_w[l].reshape(1, dv)
        diff_w = diff_norm_w[l].reshape(1, dh)
        lamv = jnp.concatenate([_pad_lanes(v) for v in (lambda_q1[l], lambda_k1[l], lambda_q2[l], lambda_k2[l])]
                               + [jnp.zeros((SUBLANES - 4, LANES), F32)], axis=0)
        ln1 = (ln1_g[l].reshape(1, d_model), ln1_b[l].reshape(1, d_model))
        ln2 = (ln2_g[l].reshape(1, d_model), ln2_b[l].reshape(1, d_model))
        proj = functools.partial(_in_proj, apply_ln=first, c_qkv=c_qkv, c_z=c_z, c_b=c_b, q_scale=q_scale,
                                 rope_half=rope_dim // 2)
        mix_out = functools.partial(_out_proj, apply_ln=first, alpha=alpha, n_groups=n_groups, n_exp=n_exp)
        heads_a = dict(n_heads=n_heads_a, dk=dk, dv=dv)
        heads_b = dict(n_heads=n_heads_b, d_map=d_map, lam_init=lam_init)

        qkv, z, ba, q16, k32, k16, v32, v16 = proj(xp, in_g, in_b, w_in_l, tabs_p)
        shape3 = lambda a: a.reshape(bp, tp, a.shape[1])
        o_a, s_new = _gdn_prompt(shape3(qkv), shape3(z), shape3(ba), conv_w[l], alog_pad, dt_pad, gdn_w, **heads_a)
        o_bp = _attn_prompt(shape3(q16), shape3(k16), shape3(v16), lamv, diff_w, **heads_b)
        x1, x1b, gates = mix_out(o_a.reshape(bp * tp, c_z), o_bp.reshape(bp * tp, c_b), xp, in_g, in_b,
                                 w_out_l, *ln1, w_r, b_r)
        conv_rows = shape3(qkv)[:, tp - (kw - 1):, :]
        if tp < kw - 1:
            conv_rows = jnp.pad(shape3(qkv), ((0, 0), (kw - 1 - tp, 0), (0, 0)))
        xp = _moe(x1b, x1, gates, wg, wu, wd, *ln2, alpha=alpha, n_groups=n_groups)
        for i, a in enumerate((k32.reshape(bp, tp, n_heads_b, dh), v32.reshape(bp, tp, n_heads_b, dh),
                               s_new, conv_rows)):
            outs[i].append(a)

        qkv, z, ba, q16, k32, _, v32, _ = proj(xs, in_g, in_b, w_in_f32, tabs_s)
        o_a, s_new, c_new = _gdn_step(qkv, state_conv[l].reshape(bs, (kw - 1) * c_qkv), z, ba, conv_w[l],
                                      alog_pad, dt_pad, gdn_w, state_delta[l], **heads_a)
        o_bs = _attn_decode(q16, k32, v32, cache_k, cache_v, page_table, l, lamv, diff_w, **heads_b)
        x1, x1b, gates = mix_out(o_a, o_bs, xs, in_g, in_b, w_out[l], *ln1, w_r, b_r)
        xs = _moe(x1b, x1, gates, wg, wu, wd, *ln2, alpha=alpha, n_groups=n_groups)
        for i, a in enumerate((k32.reshape(bs, ts, n_heads_b, dh), v32.reshape(bs, ts, n_heads_b, dh),
                               s_new, c_new.reshape(bs, kw - 1, c_qkv))):
            outs[4 + i].append(a)

    return (xp.reshape(bp, tp, d_model), xs.reshape(bs, ts, d_model)) + tuple(jnp.stack(o) for o in outs)
```
